```python
import math
import jax
import jax.numpy as jnp
from jax import lax
import numpy as np

D_MODEL = 2048
BATCH = 1
SEQ = 8192
DEPTH = 4

GRID_W = 64
CTX_LEN = 256
EPS = 1e-6
N_MOD = 6

D_MIX = D_MODEL
SSD_HEAD_DIM = 64
SSD_WIDTH = 3 * D_MIX // 8
SSD_HEADS = SSD_WIDTH // SSD_HEAD_DIM
SSD_GROUPS = 2
SSD_STATE = 128
SSD_CONV = 5
SSD_CHUNK = 128
SSD_XBC = SSD_WIDTH + 2 * SSD_GROUPS * SSD_STATE
MLA_NOPE = 128
MLA_ROPE = 64
MLA_V = 128
MLA_QK = MLA_NOPE + MLA_ROPE
MLA_WIDTH = 3 * D_MIX // 8
MLA_HEADS = MLA_WIDTH // MLA_V
MLA_Q_RANK = D_MODEL // 4
MLA_KV_RANK = D_MODEL // 8
Q_BLOCK = 128
ROPE_THETA = 10000.0
HY_WIDTH = D_MIX - SSD_WIDTH - MLA_WIDTH
HY_ORDER = 2
HY_SHORT = 3
HY_BANDS = 16
HY_EMB = 1 + 2 * HY_BANDS
HY_FFN = 64
HY_TARGET = 1e-2
HY_FAST_DECAY_PCT = 0.3
HY_SLOW_DECAY_PCT = 1.5
N_EXPERTS = 16
EXPERT_FF = D_MODEL // 2
CAPACITY_FACTOR = 2

IN_SPLITS = (SSD_WIDTH, SSD_XBC, 2 * SSD_HEADS, MLA_Q_RANK, MLA_KV_RANK, MLA_ROPE, (HY_ORDER + 1) * HY_WIDTH)
IN_COLS = sum(IN_SPLITS)

kernel_name = 'hybrid_ssd_mla_hyena_ecmoe_dit'


def rms_norm(x, w):
    xf = x.astype(jnp.float32)
    xf = xf * lax.rsqrt(jnp.mean(xf * xf, axis=-1, keepdims=True) + EPS)
    return (xf * w.astype(jnp.float32)).astype(x.dtype)


def modulate(h, shift, scale):
    return h * (1 + scale) + shift


def centred_dwconv(x, w, b):
    k = w.shape[0]
    y = lax.conv_general_dilated(x, w[:, None, :].astype(x.dtype), window_strides=(1,),
                                 padding=[(k // 2, k // 2)], dimension_numbers=('NWC', 'WIO', 'NWC'),
                                 feature_group_count=x.shape[-1])
    return y + b


def split_cols(y):
    return jnp.split(y, np.cumsum(IN_SPLITS)[:-1].tolist(), axis=-1)


def axial_rope(rows):
    r, col = jnp.meshgrid(jnp.arange(rows), jnp.arange(GRID_W), indexing='ij')
    per_axis = MLA_ROPE // 2
    inv = ROPE_THETA ** (-jnp.arange(0, per_axis, 2, dtype=jnp.float32) / per_axis)
    ang = jnp.concatenate([r.reshape(-1, 1) * inv, col.reshape(-1, 1) * inv], axis=-1)
    return jnp.cos(ang), jnp.sin(ang)


def apply_rope(x, cos, sin):
    x1, x2 = jnp.split(x, 2, axis=-1)
    cos = cos.astype(x.dtype)
    sin = sin.astype(x.dtype)
    return jnp.concatenate([x1 * cos - x2 * sin, x1 * sin + x2 * cos], axis=-1)


def ssd_chunked(xs, dt, a, bm, cm, h0, need_y):
    b, L, H, P = xs.shape
    nc = L // SSD_CHUNK
    shp = (b, nc, SSD_CHUNK, H)
    xc = xs.reshape(*shp, P)
    bc = bm.reshape(*shp, SSD_STATE)
    cc = cm.reshape(*shp, SSD_STATE)
    dtc = dt.reshape(shp)
    acum = jnp.cumsum(dtc * a, axis=2)
    to_end = jnp.exp(acum[:, :, -1:] - acum)
    states = jnp.einsum('bcjhn,bcjh,bcjhp->bchpn', bc, to_end * dtc, xc)
    chunk_decay = jnp.exp(acum[:, :, -1])

    def step(h, inp):
        s, d = inp
        return h * d[:, :, None, None] + s, h

    h_final, h_in = lax.scan(step, h0, (jnp.moveaxis(states, 1, 0), jnp.moveaxis(chunk_decay, 1, 0)))
    if not need_y:
        return None, h_final
    h_in = jnp.moveaxis(h_in, 0, 1)
    seg = acum[:, :, :, None, :] - acum[:, :, None, :, :]
    lower = jnp.tril(jnp.ones((SSD_CHUNK, SSD_CHUNK), dtype=bool))[:, :, None]
    decay = jnp.exp(jnp.where(lower, seg, -jnp.inf))
    scores = jnp.einsum('bcihn,bcjhn->bcijh', cc, bc) * decay
    y_diag = jnp.einsum('bcijh,bcjh,bcjhp->bcihp', scores, dtc, xc)
    y_off = jnp.einsum('bcihn,bchpn,bcih->bcihp', cc, h_in, jnp.exp(acum))
    return (y_diag + y_off).reshape(b, L, H, P), h_final


def ssd_branch(z, xbc, dt_raw, conv_w, conv_b, dt_bias, a_log, d_skip, norm_w, h0_f, h0_b, need_out):
    b, L, _ = xbc.shape
    xbc = jax.nn.silu(centred_dwconv(xbc, conv_w, conv_b))
    xs, bm, cm = jnp.split(xbc, [SSD_WIDTH, SSD_WIDTH + SSD_GROUPS * SSD_STATE], axis=-1)
    xs = xs.reshape(b, L, SSD_HEADS, SSD_HEAD_DIM)
    rep = SSD_HEADS // SSD_GROUPS
    bm = jnp.repeat(bm.reshape(b, L, SSD_GROUPS, SSD_STATE), rep, axis=2)
    cm = jnp.repeat(cm.reshape(b, L, SSD_GROUPS, SSD_STATE), rep, axis=2)
    dt = jax.nn.softplus(dt_raw.reshape(b, L, 2, SSD_HEADS) + dt_bias)
    a = -jnp.exp(a_log)
    y_f, h_f = ssd_chunked(xs, dt[:, :, 0], a[0], bm, cm, h0_f, need_out)
    y_b, h_b = ssd_chunked(jnp.flip(xs, 1), jnp.flip(dt[:, :, 1], 1), a[1], jnp.flip(bm, 1),
                           jnp.flip(cm, 1), h0_b, need_out)
    if not need_out:
        return None, h_f, h_b
    y = y_f + jnp.flip(y_b, 1) + d_skip[:, None] * xs
    y = rms_norm(y.reshape(b, L, SSD_WIDTH) * jax.nn.silu(z), norm_w)
    return y, h_f, h_b


def mla_q(cq, q_norm_w, w_uq, cos, sin):
    b, L, _ = cq.shape
    q = (rms_norm(cq, q_norm_w) @ w_uq).reshape(b, L, MLA_HEADS, MLA_QK)
    q_nope, q_rope = jnp.split(q, [MLA_NOPE], axis=-1)
    if cos is not None:
        q_rope = apply_rope(q_rope, cos[:, None], sin[:, None])
    return jnp.concatenate([q_nope, q_rope], axis=-1)


def mla_kv(ckv, kr, kv_norm_w, w_ukv, cos, sin):
    b, L, _ = ckv.shape
    kv = (rms_norm(ckv, kv_norm_w) @ w_ukv).reshape(b, L, MLA_HEADS, MLA_NOPE + MLA_V)
    k_nope, v = jnp.split(kv, [MLA_NOPE], axis=-1)
    if cos is not None:
        kr = apply_rope(kr, cos, sin)
    k = jnp.concatenate([k_nope, jnp.broadcast_to(kr[:, :, None], (b, L, MLA_HEADS, MLA_ROPE))], axis=-1)
    return k, v


def attend(q, k, v):
    s = jnp.einsum('bqhd,bkhd->bhqk', q, k, preferred_element_type=jnp.float32) / math.sqrt(MLA_QK)
    p = jax.nn.softmax(s, axis=-1).astype(v.dtype)
    return jnp.einsum('bhqk,bkhd->bqhd', p, v)


def blocked_attend(q, k, v):
    b, L, h, dqk = q.shape
    nb = L // Q_BLOCK
    qb = jnp.moveaxis(q.reshape(b, nb, Q_BLOCK, h, dqk), 1, 0)
    out = lax.map(lambda blk: attend(blk, k, v), qb)
    return jnp.moveaxis(out, 0, 1).reshape(b, L, h * MLA_V)


def hyena_filters(L, w1, b1, w2, b2, w3, b3, w_out, freq):
    f32 = jnp.float32
    t = jnp.linspace(0.0, 1.0, L, dtype=f32)[:, None]
    bands = jnp.linspace(1e-4, HY_BANDS - 1, HY_BANDS, dtype=f32)
    ang = 2 * math.pi * jnp.arange(L, dtype=f32)[:, None] * bands / L
    feats = jnp.concatenate([t, jnp.cos(ang), -jnp.sin(ang)], axis=-1)
    fr = freq.astype(f32)
    hdn = jnp.sin(fr * (feats @ w1.astype(f32) + b1.astype(f32)))
    hdn = jnp.sin(fr * (hdn @ w2.astype(f32) + b2.astype(f32)))
    hdn = jnp.sin(fr * (hdn @ w3.astype(f32) + b3.astype(f32)))
    h = (hdn @ w_out.astype(f32)).reshape(L, HY_ORDER, 2, HY_WIDTH)
    deltas = jnp.abs(jnp.linspace(math.log(HY_TARGET) / HY_SLOW_DECAY_PCT,
                                  math.log(HY_TARGET) / HY_FAST_DECAY_PCT, HY_WIDTH, dtype=f32))
    h = h * jnp.exp(-t * deltas)[:, None, None, :]
    return h / jnp.sum(jnp.abs(h), axis=(0, 2), keepdims=True)


def bidir_fftconv(u, h_fwd, h_bwd):
    L = u.shape[1]
    k = jnp.concatenate([h_fwd, jnp.zeros_like(h_fwd[:1]), h_bwd[:0:-1]], axis=0)
    kf = jnp.fft.rfft(k, n=2 * L, axis=0)
    uf = jnp.fft.rfft(u.astype(jnp.float32), n=2 * L, axis=1)
    return jnp.fft.irfft(uf * kf, n=2 * L, axis=1)[:, :L].astype(u.dtype)


def hyena_branch(u, short_w, short_b, w1, b1, w2, b2, w3, b3, w_out, freq, bias):
    L = u.shape[1]
    u = centred_dwconv(u, short_w, short_b)
    v, x1, x2 = jnp.split(u, HY_ORDER + 1, axis=-1)
    h = hyena_filters(L, w1, b1, w2, b2, w3, b3, w_out, freq)
    z = v
    for o, gate in enumerate((x1, x2)):
        z = gate * (bidir_fftconv(z, h[:, o, 0], h[:, o, 1]) + bias[o] * z)
    return z


def token_mixer(h_lat, h_ctx, cos, sin, p, need_ctx_out):
    b = h_lat.shape[0]
    z_l, xbc_l, dt_l, cq_l, ckv_l, kr_l, hy_l = split_cols(h_lat @ p['w_in'])
    z_c, xbc_c, dt_c, cq_c, ckv_c, kr_c, hy_c = split_cols(h_ctx @ p['w_in'])
    ssd_p = (p['ssd_conv_w'], p['ssd_conv_b'], p['ssd_dt_bias'], p['ssd_a_log'], p['ssd_d'], p['ssd_norm_w'])
    h0 = jnp.zeros((b, SSD_HEADS, SSD_HEAD_DIM, SSD_STATE), h_lat.dtype)
    y_ssd_c, hf, hb = ssd_branch(z_c, xbc_c, dt_c, *ssd_p, h0, h0, need_ctx_out)
    y_ssd_l, _, _ = ssd_branch(z_l, xbc_l, dt_l, *ssd_p, hf, hb, True)
    k_l, v_l = mla_kv(ckv_l, kr_l, p['mla_kv_norm_w'], p['mla_w_ukv'], cos, sin)
    k_c, v_c = mla_kv(ckv_c, kr_c, p['mla_kv_norm_w'], p['mla_w_ukv'], None, None)
    q_l = mla_q(cq_l, p['mla_q_norm_w'], p['mla_w_uq'], cos, sin)
    y_mla_l = blocked_attend(q_l, jnp.concatenate([k_l, k_c], axis=1), jnp.concatenate([v_l, v_c], axis=1))
    hy_p = (p['hy_short_w'], p['hy_short_b'], p['hy_w1'], p['hy_b1'], p['hy_w2'], p['hy_b2'],
            p['hy_w3'], p['hy_b3'], p['hy_w_out'], p['hy_freq'], p['hy_bias'])
    y_hy_l = hyena_branch(hy_l, *hy_p)
    out_l = jnp.concatenate([y_ssd_l, y_mla_l, y_hy_l], axis=-1) @ p['w_out']
    if not need_ctx_out:
        return out_l, None
    q_c = mla_q(cq_c, p['mla_q_norm_w'], p['mla_w_uq'], None, None)
    y_mla_c = attend(q_c, k_c, v_c).reshape(b, h_ctx.shape[1], MLA_WIDTH)
    y_hy_c = hyena_branch(hy_c, *hy_p)
    out_c = jnp.concatenate([y_ssd_c, y_mla_c, y_hy_c], axis=-1) @ p['w_out']
    return out_l, out_c


def ec_moe(h, router, w1, w3, w2):
    b, L, d = h.shape
    cap = CAPACITY_FACTOR * L // N_EXPERTS
    logits = jnp.einsum('bld,de->ble', h, router, preferred_element_type=jnp.float32)
    aff = jax.nn.softmax(logits, axis=-1)
    gates, idx = lax.top_k(jnp.swapaxes(aff, 1, 2), cap)
    xs = jax.vmap(lambda hb, ib: hb[ib])(h, idx)
    hid = jax.nn.silu(jnp.einsum('becd,edf->becf', xs, w1)) * jnp.einsum('becd,edf->becf', xs, w3)
    out = jnp.einsum('becf,efd->becd', hid, w2) * gates[..., None].astype(h.dtype)
    return jax.vmap(lambda ob, ib: jnp.zeros((L, d), ob.dtype).at[ib.reshape(-1)].add(ob.reshape(-1, d)))(out, idx)


def setup_inputs(seed: int = 0) -> dict:
    key = jax.random.key(seed)
    ks = iter(jax.random.split(key, 48))
    f32 = jnp.float32
    D = D_MODEL

    def nrm(shape, scale):
        return jax.random.normal(next(ks), shape, f32) * scale

    def gain(shape):
        return 1.0 + nrm(shape, 0.1)

    dt0 = jnp.exp(jax.random.uniform(next(ks), (DEPTH, 2, SSD_HEADS), f32, math.log(1e-3), math.log(1e-1)))
    ssd_dt_bias = dt0 + jnp.log(-jnp.expm1(-dt0))
    ssd_a_log = jnp.log(jax.random.uniform(next(ks), (DEPTH, 2, SSD_HEADS), f32, 1.0, 16.0))
    return {
        'x': nrm((BATCH, SEQ, D), 1.0),
        'c': nrm((BATCH, D), 1.0),
        'ctx': nrm((BATCH, CTX_LEN, D), 1.0),
        'c_ctx': nrm((D,), 1.0),
        'w_mod': nrm((DEPTH, D, N_MOD * D), 0.5 * D ** -0.5),
        'b_mod': nrm((DEPTH, N_MOD * D), 0.02),
        'norm1_w': gain((DEPTH, D)),
        'norm2_w': gain((DEPTH, D)),
        'w_in': nrm((DEPTH, D, IN_COLS), D ** -0.5),
        'w_out': nrm((DEPTH, D_MIX, D), D_MIX ** -0.5),
        'ssd_conv_w': nrm((DEPTH, SSD_CONV, SSD_XBC), SSD_CONV ** -0.5),
        'ssd_conv_b': nrm((DEPTH, SSD_XBC), 0.02),
        'ssd_dt_bias': ssd_dt_bias,
        'ssd_a_log': ssd_a_log,
        'ssd_d': gain((DEPTH, SSD_HEADS)),
        'ssd_norm_w': gain((DEPTH, SSD_WIDTH)),
        'mla_q_norm_w': gain((DEPTH, MLA_Q_RANK)),
        'mla_w_uq': nrm((DEPTH, MLA_Q_RANK, MLA_HEADS * MLA_QK), MLA_Q_RANK ** -0.5),
        'mla_kv_norm_w': gain((DEPTH, MLA_KV_RANK)),
        'mla_w_ukv': nrm((DEPTH, MLA_KV_RANK, MLA_HEADS * (MLA_NOPE + MLA_V)), MLA_KV_RANK ** -0.5),
        'hy_short_w': nrm((DEPTH, HY_SHORT, (HY_ORDER + 1) * HY_WIDTH), HY_SHORT ** -0.5),
        'hy_short_b': nrm((DEPTH, (HY_ORDER + 1) * HY_WIDTH), 0.02),
        'hy_w1': nrm((DEPTH, HY_EMB, HY_FFN), HY_EMB ** -0.5),
        'hy_b1': nrm((DEPTH, HY_FFN), 0.02),
        'hy_w2': nrm((DEPTH, HY_FFN, HY_FFN), HY_FFN ** -0.5),
        'hy_b2': nrm((DEPTH, HY_FFN), 0.02),
        'hy_w3': nrm((DEPTH, HY_FFN, HY_FFN), HY_FFN ** -0.5),
        'hy_b3': nrm((DEPTH, HY_FFN), 0.02),
        'hy_w_out': nrm((DEPTH, HY_FFN, HY_ORDER * 2 * HY_WIDTH), HY_FFN ** -0.5),
        'hy_freq': gain((DEPTH, HY_FFN)),
        'hy_bias': nrm((DEPTH, HY_ORDER, HY_WIDTH), 0.5),
        'moe_router': nrm((DEPTH, D, N_EXPERTS), D ** -0.5),
        'moe_w1': nrm((DEPTH, N_EXPERTS, D, EXPERT_FF), D ** -0.5),
        'moe_w3': nrm((DEPTH, N_EXPERTS, D, EXPERT_FF), D ** -0.5),
        'moe_w2': nrm((DEPTH, N_EXPERTS, EXPERT_FF, D), EXPERT_FF ** -0.5),
        'final_norm_w': gain((D,)),
    }


def reference(x, c, ctx, c_ctx, w_mod, b_mod, norm1_w, norm2_w, w_in, w_out, ssd_conv_w, ssd_conv_b,
              ssd_dt_bias, ssd_a_log, ssd_d, ssd_norm_w, mla_q_norm_w, mla_w_uq, mla_kv_norm_w, mla_w_ukv,
              hy_short_w, hy_short_b, hy_w1, hy_b1, hy_w2, hy_b2, hy_w3, hy_b3, hy_w_out, hy_freq, hy_bias,
              moe_router, moe_w1, moe_w3, moe_w2, final_norm_w):
    rows = x.shape[1] // GRID_W
    cos, sin = axial_rope(rows)
    silu_c = jax.nn.silu(c)
    silu_cc = jax.nn.silu(c_ctx)
    for i in range(DEPTH):
        last = i == DEPTH - 1
        mod_l = jnp.split((silu_c @ w_mod[i] + b_mod[i])[:, None, :], N_MOD, axis=-1)
        mod_c = jnp.split(silu_cc @ w_mod[i] + b_mod[i], N_MOD, axis=-1)
        p = {
            'w_in': w_in[i], 'w_out': w_out[i],
            'ssd_conv_w': ssd_conv_w[i], 'ssd_conv_b': ssd_conv_b[i], 'ssd_dt_bias': ssd_dt_bias[i],
            'ssd_a_log': ssd_a_log[i], 'ssd_d': ssd_d[i], 'ssd_norm_w': ssd_norm_w[i],
            'mla_q_norm_w': mla_q_norm_w[i], 'mla_w_uq': mla_w_uq[i],
            'mla_kv_norm_w': mla_kv_norm_w[i], 'mla_w_ukv': mla_w_ukv[i],
            'hy_short_w': hy_short_w[i], 'hy_short_b': hy_short_b[i], 'hy_w1': hy_w1[i], 'hy_b1': hy_b1[i],
            'hy_w2': hy_w2[i], 'hy_b2': hy_b2[i], 'hy_w3': hy_w3[i], 'hy_b3': hy_b3[i],
            'hy_w_out': hy_w_out[i], 'hy_freq': hy_freq[i], 'hy_bias': hy_bias[i],
        }
        h_l = modulate(rms_norm(x, norm1_w[i]), mod_l[0], mod_l[1])
        h_c = modulate(rms_norm(ctx, norm1_w[i]), mod_c[0], mod_c[1])
        y_l, y_c = token_mixer(h_l, h_c, cos, sin, p, not last)
        x = x + mod_l[2] * y_l
        x = x + mod_l[5] * ec_moe(modulate(rms_norm(x, norm2_w[i]), mod_l[3], mod_l[4]),
                                  moe_router[i], moe_w1[i], moe_w3[i], moe_w2[i])
        if not last:
            ctx = ctx + mod_c[2] * y_c
            ctx = ctx + mod_c[5] * ec_moe(modulate(rms_norm(ctx, norm2_w[i]), mod_c[3], mod_c[4]),
                                          moe_router[i], moe_w1[i], moe_w3[i], moe_w2[i])
    return rms_norm(x, final_norm_w)
```

```python
import functools
import math

import jax
import jax.numpy as jnp
import numpy as np
from jax import lax
from jax.experimental import pallas as pl
from jax.experimental.pallas import tpu as pltpu

D_MODEL = 2048
SEQ = 8192
DEPTH = 4
GRID_W = 64
CTX_LEN = 256
T_ALL = SEQ + CTX_LEN
EPS = 1e-6
N_MOD = 6

SSD_HEAD_DIM = 64
SSD_WIDTH = 768
SSD_HEADS = 12
SSD_GROUPS = 2
SSD_STATE = 128
SSD_CONV = 5
SSD_CHUNK = 128
SSD_XBC = SSD_WIDTH + 2 * SSD_GROUPS * SSD_STATE
MLA_NOPE = 128
MLA_ROPE = 64
MLA_V = 128
MLA_QK = MLA_NOPE + MLA_ROPE
MLA_WIDTH = 768
MLA_HEADS = 6
MLA_Q_RANK = 512
MLA_KV_RANK = 256
ROPE_THETA = 10000.0
HY_WIDTH = 512
HY_ORDER = 2
HY_SHORT = 3
HY_BANDS = 16
HY_EMB = 1 + 2 * HY_BANDS
HY_FFN = 64
HY_TARGET = 1e-2
HY_FAST_DECAY_PCT = 0.3
HY_SLOW_DECAY_PCT = 1.5
N_EXPERTS = 16
EXPERT_FF = 1024
CAPACITY_FACTOR = 2

P_HY = 0
P_Z = 1536
P_CKV = 2304
P_XBC = 2560
P_KR = 3840
P_DT = 3968
P_CQ = 4096
P_COLS = 4608
QK_PAD = 256

F32 = jnp.float32
BF16 = jnp.bfloat16
VMEM_LIMIT = 56 * 1024 * 1024


def _cparams(*sem):
    return pltpu.CompilerParams(dimension_semantics=sem, vmem_limit_bytes=VMEM_LIMIT)


def _split3(a):
    hi = a.astype(BF16)
    r = a - hi.astype(F32)
    mid = r.astype(BF16)
    lo = (r - mid.astype(F32)).astype(BF16)
    return hi, mid, lo


def _dot(a, b):
    return jnp.dot(a, b, preferred_element_type=F32)


def _dot_hp(a, b):
    a1, a2, a3 = _split3(a)
    b1, b2, b3 = _split3(b)
    return (_dot(a1, b1) + (_dot(a1, b2) + _dot(a2, b1))
            + (_dot(a2, b2) + _dot(a1, b3) + _dot(a3, b1)))


def _silu(v):
    return v * (1.0 / (1.0 + jnp.exp(-v)))


def _mod_kernel(c_ref, w_ref, b_ref, o_ref):
    s = _silu(c_ref[...])
    o_ref[0] = _dot_hp(s, w_ref[0]) + b_ref[0]


def modulation(cvec, w_mod, b_mod):
    tn = 1024
    return pl.pallas_call(
        _mod_kernel,
        grid=(DEPTH, N_MOD * D_MODEL // tn),
        in_specs=[pl.BlockSpec((8, D_MODEL), lambda l, j: (0, 0)),
                  pl.BlockSpec((1, D_MODEL, tn), lambda l, j: (l, 0, j)),
                  pl.BlockSpec((1, 1, tn), lambda l, j: (l, 0, j))],
        out_specs=pl.BlockSpec((1, 8, tn), lambda l, j: (l, 0, j)),
        out_shape=jax.ShapeDtypeStruct((DEPTH, 8, N_MOD * D_MODEL), F32),
        compiler_params=_cparams("parallel", "parallel"),
        name="modulation",
    )(cvec, w_mod, b_mod.reshape(DEPTH, 1, N_MOD * D_MODEL))


def _norm_mod(x, nw, mod_ref, row0, k_shift, k_scale):
    n = x.shape[0]
    xn = x * lax.rsqrt(jnp.mean(x * x, axis=-1, keepdims=True) + EPS) * nw
    is_ctx = (row0 + lax.broadcasted_iota(jnp.int32, (n, 1), 0)) >= SEQ
    sl_shift = slice(k_shift * D_MODEL, (k_shift + 1) * D_MODEL)
    sl_scale = slice(k_scale * D_MODEL, (k_scale + 1) * D_MODEL)
    shift = jnp.where(is_ctx, mod_ref[1:2, sl_shift], mod_ref[0:1, sl_shift])
    scale = jnp.where(is_ctx, mod_ref[1:2, sl_scale], mod_ref[0:1, sl_scale])
    return xn * (1.0 + scale) + shift


IN_TM = 768
IN_TN = 768


def _inproj_kernel(x_ref, mod_ref, nw_ref, w_ref, o_ref, h_ref):
    @pl.when(pl.program_id(1) == 0)
    def _():
        h = _norm_mod(x_ref[...], nw_ref[...], mod_ref, pl.program_id(0) * IN_TM, 0, 1)
        h_ref[...] = h.astype(BF16)

    o_ref[...] = _dot(h_ref[...], w_ref[...])


def in_projection(x, mod, nw, w):
    return pl.pallas_call(
        _inproj_kernel,
        grid=(T_ALL // IN_TM, P_COLS // IN_TN),
        in_specs=[pl.BlockSpec((IN_TM, D_MODEL), lambda i, j: (i, 0)),
                  pl.BlockSpec((8, N_MOD * D_MODEL), lambda i, j: (0, 0)),
                  pl.BlockSpec((1, D_MODEL), lambda i, j: (0, 0)),
                  pl.BlockSpec((D_MODEL, IN_TN), lambda i, j: (0, j))],
        out_specs=pl.BlockSpec((IN_TM, IN_TN), lambda i, j: (i, j)),
        out_shape=jax.ShapeDtypeStruct((T_ALL, P_COLS), F32),
        scratch_shapes=[pltpu.VMEM((IN_TM, D_MODEL), BF16)],
        compiler_params=_cparams("parallel", "arbitrary"),
        name="in_projection",
    )(x, mod, nw, w)


MLA_TM = 768


def _rope(v, c2, s2):
    return v * c2 + pltpu.roll(v, 64, 1) * s2


def _mla_proj_kernel(cq_ref, ckv_ref, kr_ref, c2_ref, s2_ref, qnw_ref, kvnw_ref, wq_ref, wkv_ref,
                     q_ref, k_ref, v_ref):
    c2 = c2_ref[...]
    s2 = s2_ref[...]
    cq = cq_ref[...]
    cqn = cq * lax.rsqrt(jnp.mean(cq * cq, axis=-1, keepdims=True) + EPS) * qnw_ref[...]
    q = _dot(cqn.astype(BF16), wq_ref[...])
    ckv = ckv_ref[...]
    ckvn = ckv * lax.rsqrt(jnp.mean(ckv * ckv, axis=-1, keepdims=True) + EPS) * kvnw_ref[...]
    kv = _dot(ckvn.astype(BF16), wkv_ref[...])
    kr = _rope(kr_ref[...], c2, s2).astype(BF16)
    for h in range(MLA_HEADS):
        b = h * QK_PAD
        q_ref[h, :, 0:MLA_NOPE] = q[:, b:b + MLA_NOPE].astype(BF16)
        q_ref[h, :, MLA_NOPE:QK_PAD] = _rope(q[:, b + MLA_NOPE:b + QK_PAD], c2, s2).astype(BF16)
        k_ref[h, :, 0:MLA_NOPE] = kv[:, b:b + MLA_NOPE].astype(BF16)
        k_ref[h, :, MLA_NOPE:QK_PAD] = kr
        v_ref[h] = kv[:, b + MLA_NOPE:b + QK_PAD].astype(BF16)


def mla_projection(p, c2, s2, qnw, kvnw, wq, wkv):
    rows = lambda i: (i, 0)
    const = lambda i: (0, 0)
    return pl.pallas_call(
        _mla_proj_kernel,
        grid=(T_ALL // MLA_TM,),
        in_specs=[pl.BlockSpec((MLA_TM, MLA_Q_RANK), lambda i: (i, P_CQ // MLA_Q_RANK)),
                  pl.BlockSpec((MLA_TM, MLA_KV_RANK), lambda i: (i, P_CKV // MLA_KV_RANK)),
                  pl.BlockSpec((MLA_TM, 128), lambda i: (i, P_KR // 128)),
                  pl.BlockSpec((MLA_TM, 128), rows),
                  pl.BlockSpec((MLA_TM, 128), rows),
                  pl.BlockSpec((1, MLA_Q_RANK), const),
                  pl.BlockSpec((1, MLA_KV_RANK), const),
                  pl.BlockSpec((MLA_Q_RANK, MLA_HEADS * QK_PAD), const),
                  pl.BlockSpec((MLA_KV_RANK, MLA_HEADS * QK_PAD), const)],
        out_specs=[pl.BlockSpec((MLA_HEADS, MLA_TM, QK_PAD), lambda i: (0, i, 0)),
                   pl.BlockSpec((MLA_HEADS, MLA_TM, QK_PAD), lambda i: (0, i, 0)),
                   pl.BlockSpec((MLA_HEADS, MLA_TM, MLA_V), lambda i: (0, i, 0))],
        out_shape=[jax.ShapeDtypeStruct((MLA_HEADS, T_ALL, QK_PAD), BF16),
                   jax.ShapeDtypeStruct((MLA_HEADS, T_ALL, QK_PAD), BF16),
                   jax.ShapeDtypeStruct((MLA_HEADS, T_ALL, MLA_V), BF16)],
        compiler_params=_cparams("parallel"),
        name="mla_projection",
    )(p, p, p, c2, s2, qnw, kvnw, wq, wkv)


ATT_TQ = 256
ATT_TK = 512
ATT_SCALE = 1.0 / math.sqrt(MLA_QK)


def _attention_kernel(q_ref, k_ref, v_ref, o_ref):
    q = q_ref[0]

    def chunk(carry, k, v):
        m, l, acc = carry
        s = lax.dot_general(q, k, (((1,), (1,)), ((), ())), preferred_element_type=F32) * ATT_SCALE
        m_new = jnp.maximum(m, jnp.max(s, axis=-1, keepdims=True))
        alpha = jnp.exp(m - m_new)
        p = jnp.exp(s - m_new)
        l = alpha * l + jnp.sum(p, axis=-1, keepdims=True)
        acc = alpha * acc + _dot(p.astype(BF16), v)
        return m_new, l, acc

    def body(c, carry):
        start = pl.multiple_of(c * ATT_TK, ATT_TK)
        return chunk(carry, k_ref[0, pl.ds(start, ATT_TK), :], v_ref[0, pl.ds(start, ATT_TK), :])

    init = (jnp.full((ATT_TQ, 1), -1e30, F32), jnp.zeros((ATT_TQ, 1), F32), jnp.zeros((ATT_TQ, MLA_V), F32))
    n_latent_chunks = jnp.where(pl.program_id(1) * ATT_TQ >= SEQ, 0, SEQ // ATT_TK)
    carry = lax.fori_loop(0, n_latent_chunks, body, init)
    m, l, acc = chunk(carry, k_ref[0, SEQ:T_ALL, :], v_ref[0, SEQ:T_ALL, :])
    o_ref[...] = acc / l


def attention(q, k, v):
    return pl.pallas_call(
        _attention_kernel,
        grid=(MLA_HEADS, T_ALL // ATT_TQ),
        in_specs=[pl.BlockSpec((1, ATT_TQ, QK_PAD), lambda h, i: (h, i, 0)),
                  pl.BlockSpec((1, T_ALL, QK_PAD), lambda h, i: (h, 0, 0)),
                  pl.BlockSpec((1, T_ALL, MLA_V), lambda h, i: (h, 0, 0))],
        out_specs=pl.BlockSpec((ATT_TQ, MLA_V), lambda h, i: (i, h)),
        out_shape=jax.ShapeDtypeStruct((T_ALL, MLA_WIDTH), F32),
        compiler_params=_cparams("parallel", "parallel"),
        name="attention",
    )(q, k, v)


OUT_TM = 384


def _outproj_kernel(ys_ref, ym_ref, yh_ref, x_ref, mod_ref, nw_ref, w_ref, r_ref, x1_ref, h2_ref, aff_ref):
    y = _dot(ys_ref[...].astype(BF16), w_ref[0:SSD_WIDTH, :])
    y += _dot(ym_ref[...].astype(BF16), w_ref[SSD_WIDTH:SSD_WIDTH + MLA_WIDTH, :])
    y += _dot(yh_ref[...].astype(BF16), w_ref[SSD_WIDTH + MLA_WIDTH:D_MODEL, :])
    row0 = pl.program_id(0) * OUT_TM
    is_ctx = (row0 + lax.broadcasted_iota(jnp.int32, (OUT_TM, 1), 0)) >= SEQ
    gate = jnp.where(is_ctx, mod_ref[1:2, 2 * D_MODEL:3 * D_MODEL], mod_ref[0:1, 2 * D_MODEL:3 * D_MODEL])
    x1 = x_ref[...] + gate * y
    x1_ref[...] = x1
    h2 = _norm_mod(x1, nw_ref[...], mod_ref, row0, 3, 4)
    h2_ref[...] = h2
    logits = _dot_hp(h2, r_ref[...])
    lane = lax.broadcasted_iota(jnp.int32, logits.shape, 1)
    logits = jnp.where(lane < N_EXPERTS, logits, -1e30)
    e = jnp.exp(logits - jnp.max(logits, axis=-1, keepdims=True))
    aff_ref[...] = e / jnp.sum(e, axis=-1, keepdims=True)


def out_projection(y_ssd, y_mla, y_hy, x, mod, nw, w, router):
    rows = lambda i: (i, 0)
    const = lambda i: (0, 0)
    return pl.pallas_call(
        _outproj_kernel,
        grid=(T_ALL // OUT_TM,),
        in_specs=[pl.BlockSpec((OUT_TM, SSD_WIDTH), rows),
                  pl.BlockSpec((OUT_TM, MLA_WIDTH), rows),
                  pl.BlockSpec((OUT_TM, HY_WIDTH), rows),
                  pl.BlockSpec((OUT_TM, D_MODEL), rows),
                  pl.BlockSpec((8, N_MOD * D_MODEL), const),
                  pl.BlockSpec((1, D_MODEL), const),
                  pl.BlockSpec((D_MODEL, D_MODEL), const),
                  pl.BlockSpec((D_MODEL, 128), const)],
        out_specs=[pl.BlockSpec((OUT_TM, D_MODEL), rows),
                   pl.BlockSpec((OUT_TM, D_MODEL), rows),
                   pl.BlockSpec((OUT_TM, 128), rows)],
        out_shape=[jax.ShapeDtypeStruct((T_ALL, D_MODEL), F32),
                   jax.ShapeDtypeStruct((T_ALL, D_MODEL), F32),
                   jax.ShapeDtypeStruct((T_ALL, 128), F32)],
        compiler_params=_cparams("parallel"),
        name="out_projection",
    )(y_ssd, y_mla, y_hy, x, mod, nw, w, router)


def _final_norm_kernel(x_ref, w_ref, o_ref):
    x = x_ref[...]
    o_ref[...] = x * lax.rsqrt(jnp.mean(x * x, axis=-1, keepdims=True) + EPS) * w_ref[...]


def final_norm(x, w):
    tm = 512
    return pl.pallas_call(
        _final_norm_kernel,
        grid=(SEQ // tm,),
        in_specs=[pl.BlockSpec((tm, D_MODEL), lambda i: (i, 0)), pl.BlockSpec((1, D_MODEL), lambda i: (0, 0))],
        out_specs=pl.BlockSpec((tm, D_MODEL), lambda i: (i, 0)),
        out_shape=jax.ShapeDtypeStruct((SEQ, D_MODEL), F32),
        compiler_params=_cparams("parallel"),
        name="final_norm",
    )(x, w)


def rms_norm(x, w):
    xf = x.astype(jnp.float32)
    xf = xf * lax.rsqrt(jnp.mean(xf * xf, axis=-1, keepdims=True) + EPS)
    return (xf * w.astype(jnp.float32)).astype(x.dtype)


def centred_dwconv(x, w, b):
    k = w.shape[0]
    y = lax.conv_general_dilated(x, w[:, None, :].astype(x.dtype), window_strides=(1,),
                                 padding=[(k // 2, k // 2)], dimension_numbers=('NWC', 'WIO', 'NWC'),
                                 feature_group_count=x.shape[-1])
    return y + b


def ssd_chunked(xs, dt, a, bm, cm, h0, need_y):
    b, L, H, P = xs.shape
    nc = L // SSD_CHUNK
    shp = (b, nc, SSD_CHUNK, H)
    xc = xs.reshape(*shp, P)
    bc = bm.reshape(*shp, SSD_STATE)
    cc = cm.reshape(*shp, SSD_STATE)
    dtc = dt.reshape(shp)
    acum = jnp.cumsum(dtc * a, axis=2)
    to_end = jnp.exp(acum[:, :, -1:] - acum)
    states = jnp.einsum('bcjhn,bcjh,bcjhp->bchpn', bc, to_end * dtc, xc)
    chunk_decay = jnp.exp(acum[:, :, -1])

    def step(h, inp):
        s, d = inp
        return h * d[:, :, None, None] + s, h

    h_final, h_in = lax.scan(step, h0, (jnp.moveaxis(states, 1, 0), jnp.moveaxis(chunk_decay, 1, 0)))
    if not need_y:
        return None, h_final
    h_in = jnp.moveaxis(h_in, 0, 1)
    seg = acum[:, :, :, None, :] - acum[:, :, None, :, :]
    lower = jnp.tril(jnp.ones((SSD_CHUNK, SSD_CHUNK), dtype=bool))[:, :, None]
    decay = jnp.exp(jnp.where(lower, seg, -jnp.inf))
    scores = jnp.einsum('bcihn,bcjhn->bcijh', cc, bc) * decay
    y_diag = jnp.einsum('bcijh,bcjh,bcjhp->bcihp', scores, dtc, xc)
    y_off = jnp.einsum('bcihn,bchpn,bcih->bcihp', cc, h_in, jnp.exp(acum))
    return (y_diag + y_off).reshape(b, L, H, P), h_final


def ssd_branch(z, xbc, dt_raw, conv_w, conv_b, dt_bias, a_log, d_skip, norm_w, h0_f, h0_b, need_out):
    b, L, _ = xbc.shape
    xbc = jax.nn.silu(centred_dwconv(xbc, conv_w, conv_b))
    xs, bm, cm = jnp.split(xbc, [SSD_WIDTH, SSD_WIDTH + SSD_GROUPS * SSD_STATE], axis=-1)
    xs = xs.reshape(b, L, SSD_HEADS, SSD_HEAD_DIM)
    rep = SSD_HEADS // SSD_GROUPS
    bm = jnp.repeat(bm.reshape(b, L, SSD_GROUPS, SSD_STATE), rep, axis=2)
    cm = jnp.repeat(cm.reshape(b, L, SSD_GROUPS, SSD_STATE), rep, axis=2)
    dt = jax.nn.softplus(dt_raw.reshape(b, L, 2, SSD_HEADS) + dt_bias)
    a = -jnp.exp(a_log)
    y_f, h_f = ssd_chunked(xs, dt[:, :, 0], a[0], bm, cm, h0_f, need_out)
    y_b, h_b = ssd_chunked(jnp.flip(xs, 1), jnp.flip(dt[:, :, 1], 1), a[1], jnp.flip(bm, 1),
                           jnp.flip(cm, 1), h0_b, need_out)
    if not need_out:
        return None, h_f, h_b
    y = y_f + jnp.flip(y_b, 1) + d_skip[:, None] * xs
    y = rms_norm(y.reshape(b, L, SSD_WIDTH) * jax.nn.silu(z), norm_w)
    return y, h_f, h_b


def hyena_filters(L, w1, b1, w2, b2, w3, b3, w_out, freq):
    f32 = jnp.float32
    t = jnp.linspace(0.0, 1.0, L, dtype=f32)[:, None]
    bands = jnp.linspace(1e-4, HY_BANDS - 1, HY_BANDS, dtype=f32)
    ang = 2 * math.pi * jnp.arange(L, dtype=f32)[:, None] * bands / L
    feats = jnp.concatenate([t, jnp.cos(ang), -jnp.sin(ang)], axis=-1)
    fr = freq.astype(f32)
    hdn = jnp.sin(fr * (feats @ w1.astype(f32) + b1.astype(f32)))
    hdn = jnp.sin(fr * (hdn @ w2.astype(f32) + b2.astype(f32)))
    hdn = jnp.sin(fr * (hdn @ w3.astype(f32) + b3.astype(f32)))
    h = (hdn @ w_out.astype(f32)).reshape(L, HY_ORDER, 2, HY_WIDTH)
    deltas = jnp.abs(jnp.linspace(math.log(HY_TARGET) / HY_SLOW_DECAY_PCT,
                                  math.log(HY_TARGET) / HY_FAST_DECAY_PCT, HY_WIDTH, dtype=f32))
    h = h * jnp.exp(-t * deltas)[:, None, None, :]
    return h / jnp.sum(jnp.abs(h), axis=(0, 2), keepdims=True)


def bidir_fftconv(u, h_fwd, h_bwd):
    L = u.shape[1]
    k = jnp.concatenate([h_fwd, jnp.zeros_like(h_fwd[:1]), h_bwd[:0:-1]], axis=0)
    kf = jnp.fft.rfft(k, n=2 * L, axis=0)
    uf = jnp.fft.rfft(u.astype(jnp.float32), n=2 * L, axis=1)
    return jnp.fft.irfft(uf * kf, n=2 * L, axis=1)[:, :L].astype(u.dtype)


def hyena_branch(u, short_w, short_b, w1, b1, w2, b2, w3, b3, w_out, freq, bias):
    L = u.shape[1]
    u = centred_dwconv(u, short_w, short_b)
    v, x1, x2 = jnp.split(u, HY_ORDER + 1, axis=-1)
    h = hyena_filters(L, w1, b1, w2, b2, w3, b3, w_out, freq)
    z = v
    for o, gate in enumerate((x1, x2)):
        z = gate * (bidir_fftconv(z, h[:, o, 0], h[:, o, 1]) + bias[o] * z)
    return z


def ec_moe(h, router, w1, w3, w2):
    b, L, d = h.shape
    cap = CAPACITY_FACTOR * L // N_EXPERTS
    logits = jnp.einsum('bld,de->ble', h, router, preferred_element_type=jnp.float32)
    aff = jax.nn.softmax(logits, axis=-1)
    gates, idx = lax.top_k(jnp.swapaxes(aff, 1, 2), cap)
    xs = jax.vmap(lambda hb, ib: hb[ib])(h, idx)
    hid = jax.nn.silu(jnp.einsum('becd,edf->becf', xs, w1)) * jnp.einsum('becd,edf->becf', xs, w3)
    out = jnp.einsum('becf,efd->becd', hid, w2) * gates[..., None].astype(h.dtype)
    return jax.vmap(lambda ob, ib: jnp.zeros((L, d), ob.dtype).at[ib.reshape(-1)].add(ob.reshape(-1, d)))(out, idx)


def _rope_tables():
    rows = SEQ // GRID_W
    r, col = jnp.meshgrid(jnp.arange(rows), jnp.arange(GRID_W), indexing='ij')
    per_axis = MLA_ROPE // 2
    inv = ROPE_THETA ** (-jnp.arange(0, per_axis, 2, dtype=F32) / per_axis)
    ang = jnp.concatenate([r.reshape(-1, 1) * inv, col.reshape(-1, 1) * inv], axis=-1)
    cos = jnp.concatenate([jnp.cos(ang), jnp.ones((CTX_LEN, per_axis), F32)], axis=0)
    sin = jnp.concatenate([jnp.sin(ang), jnp.zeros((CTX_LEN, per_axis), F32)], axis=0)
    z = jnp.zeros_like(cos)
    return jnp.concatenate([cos, z, cos, z], axis=-1), jnp.concatenate([-sin, z, sin, z], axis=-1)


def _permute_w_in(w_in):
    z, xbc, dt, cq, ckv, kr, hy = jnp.split(w_in, np.cumsum(
        (SSD_WIDTH, SSD_XBC, 2 * SSD_HEADS, MLA_Q_RANK, MLA_KV_RANK, MLA_ROPE)).tolist(), axis=-1)
    zeros = lambda n: jnp.zeros(w_in.shape[:-1] + (n,), w_in.dtype)
    return jnp.concatenate([hy, z, ckv, xbc, kr[..., :32], zeros(32), kr[..., 32:], zeros(32),
                            dt, zeros(128 - 2 * SSD_HEADS), cq], axis=-1).astype(BF16)


def _pad_w_uq(w_uq):
    w = w_uq.reshape(DEPTH, MLA_Q_RANK, MLA_HEADS, MLA_QK)
    zeros = jnp.zeros(w.shape[:-1] + (32,), w.dtype)
    w = jnp.concatenate([w[..., :MLA_NOPE], w[..., MLA_NOPE:MLA_NOPE + 32], zeros,
                         w[..., MLA_NOPE + 32:], zeros], axis=-1)
    return w.reshape(DEPTH, MLA_Q_RANK, MLA_HEADS * QK_PAD).astype(BF16)


def kernel(x, c, ctx, c_ctx, w_mod, b_mod, norm1_w, norm2_w, w_in, w_out, ssd_conv_w, ssd_conv_b,
           ssd_dt_bias, ssd_a_log, ssd_d, ssd_norm_w, mla_q_norm_w, mla_w_uq, mla_kv_norm_w, mla_w_ukv,
           hy_short_w, hy_short_b, hy_w1, hy_b1, hy_w2, hy_b2, hy_w3, hy_b3, hy_w_out, hy_freq, hy_bias,
           moe_router, moe_w1, moe_w3, moe_w2, final_norm_w):
    xj = jnp.concatenate([x[0], ctx[0]], axis=0)
    cvec = jnp.concatenate([c, c_ctx[None, :], jnp.zeros((6, D_MODEL), F32)], axis=0)
    mods = modulation(cvec, w_mod, b_mod)
    c2, s2 = _rope_tables()
    w_in_p = _permute_w_in(w_in)
    w_uq_p = _pad_w_uq(mla_w_uq)
    w_ukv_b = mla_w_ukv.astype(BF16)
    w_out_b = w_out.astype(BF16)
    router_p = jnp.pad(moe_router, ((0, 0), (0, 0), (0, 128 - N_EXPERTS)))

    for i in range(DEPTH):
        mod = mods[i]
        p = in_projection(xj, mod, norm1_w[i][None, :], w_in_p[i])
        q, k, v = mla_projection(p, c2, s2, mla_q_norm_w[i][None, :], mla_kv_norm_w[i][None, :],
                                 w_uq_p[i], w_ukv_b[i])
        y_mla = attention(q, k, v)

        zc = p[:, P_Z:P_Z + SSD_WIDTH]
        xbc = p[:, P_XBC:P_XBC + SSD_XBC]
        dtr = p[:, P_DT:P_DT + 2 * SSD_HEADS]
        hy = p[:, P_HY:P_HY + 3 * HY_WIDTH]
        ssd_p = (ssd_conv_w[i], ssd_conv_b[i], ssd_dt_bias[i], ssd_a_log[i], ssd_d[i], ssd_norm_w[i])
        h0 = jnp.zeros((1, SSD_HEADS, SSD_HEAD_DIM, SSD_STATE), F32)
        lat = lambda a: a[None, :SEQ]
        cx = lambda a: a[None, SEQ:]
        y_ssd_c, hf, hb = ssd_branch(cx(zc), cx(xbc), cx(dtr), *ssd_p, h0, h0, True)
        y_ssd_l, _, _ = ssd_branch(lat(zc), lat(xbc), lat(dtr), *ssd_p, hf, hb, True)
        hy_p = (hy_short_w[i], hy_short_b[i], hy_w1[i], hy_b1[i], hy_w2[i], hy_b2[i],
                hy_w3[i], hy_b3[i], hy_w_out[i], hy_freq[i], hy_bias[i])
        y_hy_l = hyena_branch(lat(hy), *hy_p)
        y_hy_c = hyena_branch(cx(hy), *hy_p)
        y_ssd = jnp.concatenate([y_ssd_l[0], y_ssd_c[0]], axis=0)
        y_hy = jnp.concatenate([y_hy_l[0], y_hy_c[0]], axis=0)

        x1, h2, aff = out_projection(y_ssd, y_mla, y_hy, xj, mod, norm2_w[i][None, :], w_out_b[i], router_p[i])

        moe_l = ec_moe(h2[None, :SEQ], moe_router[i], moe_w1[i], moe_w3[i], moe_w2[i])[0]
        moe_c = ec_moe(h2[None, SEQ:], moe_router[i], moe_w1[i], moe_w3[i], moe_w2[i])[0]
        g5 = jnp.concatenate([jnp.broadcast_to(mod[0:1, 5 * D_MODEL:], (SEQ, D_MODEL)),
                              jnp.broadcast_to(mod[1:2, 5 * D_MODEL:], (CTX_LEN, D_MODEL))], axis=0)
        xj = x1 + g5 * jnp.concatenate([moe_l, moe_c], axis=0)
    return final_norm(xj[:SEQ], final_norm_w[None, :])[None]
```

```python
import functools
import math

import jax
import jax.numpy as jnp
import numpy as np
from jax import lax
from jax.experimental import pallas as pl
from jax.experimental.pallas import tpu as pltpu

D_MODEL = 2048
SEQ = 8192
DEPTH = 4
GRID_W = 64
CTX_LEN = 256
T_ALL = SEQ + CTX_LEN
EPS = 1e-6
N_MOD = 6

SSD_HEAD_DIM = 64
SSD_WIDTH = 768
SSD_HEADS = 12
SSD_GROUPS = 2
SSD_STATE = 128
SSD_CONV = 5
SSD_CHUNK = 128
SSD_XBC = SSD_WIDTH + 2 * SSD_GROUPS * SSD_STATE
MLA_NOPE = 128
MLA_ROPE = 64
MLA_V = 128
MLA_QK = MLA_NOPE + MLA_ROPE
MLA_WIDTH = 768
MLA_HEADS = 6
MLA_Q_RANK = 512
MLA_KV_RANK = 256
ROPE_THETA = 10000.0
HY_WIDTH = 512
HY_ORDER = 2
HY_SHORT = 3
HY_BANDS = 16
HY_EMB = 1 + 2 * HY_BANDS
HY_FFN = 64
HY_TARGET = 1e-2
HY_FAST_DECAY_PCT = 0.3
HY_SLOW_DECAY_PCT = 1.5
N_EXPERTS = 16
EXPERT_FF = 1024
CAPACITY_FACTOR = 2

P_HY = 0
P_Z = 1536
P_CKV = 2304
P_XBC = 2560
P_KR = 3840
P_DT = 3968
P_CQ = 4096
P_COLS = 4608
QK_PAD = 256
V_PAD = 256
ATT_SCALE = 1.0 / math.sqrt(MLA_QK)
CAP_L = CAPACITY_FACTOR * SEQ // N_EXPERTS
CAP_C = CAPACITY_FACTOR * CTX_LEN // N_EXPERTS
SLOTS = 1152
N_TB = T_ALL // 128

F32 = jnp.float32
BF16 = jnp.bfloat16
VMEM_LIMIT = 56 * 1024 * 1024


def _cparams(*sem):
    return pltpu.CompilerParams(dimension_semantics=sem, vmem_limit_bytes=VMEM_LIMIT)


def _split3(a):
    hi = a.astype(BF16)
    r = a - hi.astype(F32)
    mid = r.astype(BF16)
    lo = (r - mid.astype(F32)).astype(BF16)
    return hi, mid, lo


def _dot(a, b):
    return jnp.dot(a, b, preferred_element_type=F32)


def _dot_hp(a, b):
    a1, a2, a3 = _split3(a)
    b1, b2, b3 = _split3(b)
    return (_dot(a1, b1) + (_dot(a1, b2) + _dot(a2, b1))
            + (_dot(a2, b2) + _dot(a1, b3) + _dot(a3, b1)))


def _silu(v):
    return v * (1.0 / (1.0 + jnp.exp(-v)))


def _mod_kernel(c_ref, w_ref, b_ref, o_ref):
    s = _silu(c_ref[...])
    o_ref[0] = _dot_hp(s, w_ref[0]) + b_ref[0]


def modulation(cvec, w_mod, b_mod):
    tn = 1024
    return pl.pallas_call(
        _mod_kernel,
        grid=(DEPTH, N_MOD * D_MODEL // tn),
        in_specs=[pl.BlockSpec((8, D_MODEL), lambda l, j: (0, 0)),
                  pl.BlockSpec((1, D_MODEL, tn), lambda l, j: (l, 0, j)),
                  pl.BlockSpec((1, 1, tn), lambda l, j: (l, 0, j))],
        out_specs=pl.BlockSpec((1, 8, tn), lambda l, j: (l, 0, j)),
        out_shape=jax.ShapeDtypeStruct((DEPTH, 8, N_MOD * D_MODEL), F32),
        compiler_params=_cparams("parallel", "parallel"),
        name="modulation",
    )(cvec, w_mod, b_mod.reshape(DEPTH, 1, N_MOD * D_MODEL))


def _norm_mod(x, nw, mod_ref, row0, k_shift, k_scale):
    n = x.shape[0]
    xn = x * lax.rsqrt(jnp.mean(x * x, axis=-1, keepdims=True) + EPS) * nw
    is_ctx = (row0 + lax.broadcasted_iota(jnp.int32, (n, 1), 0)) >= SEQ
    sl_shift = slice(k_shift * D_MODEL, (k_shift + 1) * D_MODEL)
    sl_scale = slice(k_scale * D_MODEL, (k_scale + 1) * D_MODEL)
    shift = jnp.where(is_ctx, mod_ref[1:2, sl_shift], mod_ref[0:1, sl_shift])
    scale = jnp.where(is_ctx, mod_ref[1:2, sl_scale], mod_ref[0:1, sl_scale])
    return xn * (1.0 + scale) + shift


IN_TM = 768
IN_TN = 768


def _inproj_kernel(x_ref, mod_ref, nw_ref, w_ref, o_ref, h_ref):
    @pl.when(pl.program_id(1) == 0)
    def _():
        h = _norm_mod(x_ref[...], nw_ref[...], mod_ref, pl.program_id(0) * IN_TM, 0, 1)
        h_ref[...] = h.astype(BF16)

    o_ref[...] = _dot(h_ref[...], w_ref[...])


def in_projection(x, mod, nw, w):
    return pl.pallas_call(
        _inproj_kernel,
        grid=(T_ALL // IN_TM, P_COLS // IN_TN),
        in_specs=[pl.BlockSpec((IN_TM, D_MODEL), lambda i, j: (i, 0)),
                  pl.BlockSpec((8, N_MOD * D_MODEL), lambda i, j: (0, 0)),
                  pl.BlockSpec((1, D_MODEL), lambda i, j: (0, 0)),
                  pl.BlockSpec((D_MODEL, IN_TN), lambda i, j: (0, j))],
        out_specs=pl.BlockSpec((IN_TM, IN_TN), lambda i, j: (i, j)),
        out_shape=jax.ShapeDtypeStruct((T_ALL, P_COLS), F32),
        scratch_shapes=[pltpu.VMEM((IN_TM, D_MODEL), BF16)],
        compiler_params=_cparams("parallel", "arbitrary"),
        name="in_projection",
    )(x, mod, nw, w)


MLA_TM = 768


def _rope(v, c2, s2):
    return v * c2 + pltpu.roll(v, 64, 1) * s2


def _mla_proj_kernel(cq_ref, ckv_ref, kr_ref, c2_ref, s2_ref, qnw_ref, kvnw_ref, wq_ref, wkv_ref,
                     q_ref, k_ref, v_ref):
    c2 = c2_ref[...]
    s2 = s2_ref[...]
    cq = cq_ref[...]
    cqn = cq * lax.rsqrt(jnp.mean(cq * cq, axis=-1, keepdims=True) + EPS) * qnw_ref[...]
    q = _dot(cqn.astype(BF16), wq_ref[...]) * ATT_SCALE
    ckv = ckv_ref[...]
    ckvn = ckv * lax.rsqrt(jnp.mean(ckv * ckv, axis=-1, keepdims=True) + EPS) * kvnw_ref[...]
    kv = _dot(ckvn.astype(BF16), wkv_ref[...])
    kr = _rope(kr_ref[...], c2, s2).astype(BF16)
    for h in range(MLA_HEADS):
        b = h * QK_PAD
        q_ref[h, :, 0:MLA_NOPE] = q[:, b:b + MLA_NOPE].astype(BF16)
        q_ref[h, :, MLA_NOPE:QK_PAD] = _rope(q[:, b + MLA_NOPE:b + QK_PAD], c2, s2).astype(BF16)
        k_ref[h, :, 0:MLA_NOPE] = kv[:, b:b + MLA_NOPE].astype(BF16)
        k_ref[h, :, MLA_NOPE:QK_PAD] = kr
        v_ref[h, :, 0:MLA_V] = kv[:, b + MLA_NOPE:b + QK_PAD].astype(BF16)
        v_ref[h, :, MLA_V:V_PAD] = jnp.ones((MLA_TM, V_PAD - MLA_V), BF16)


def mla_projection(p, c2, s2, qnw, kvnw, wq, wkv):
    rows = lambda i: (i, 0)
    const = lambda i: (0, 0)
    return pl.pallas_call(
        _mla_proj_kernel,
        grid=(T_ALL // MLA_TM,),
        in_specs=[pl.BlockSpec((MLA_TM, MLA_Q_RANK), lambda i: (i, P_CQ // MLA_Q_RANK)),
                  pl.BlockSpec((MLA_TM, MLA_KV_RANK), lambda i: (i, P_CKV // MLA_KV_RANK)),
                  pl.BlockSpec((MLA_TM, 128), lambda i: (i, P_KR // 128)),
                  pl.BlockSpec((MLA_TM, 128), rows),
                  pl.BlockSpec((MLA_TM, 128), rows),
                  pl.BlockSpec((1, MLA_Q_RANK), const),
                  pl.BlockSpec((1, MLA_KV_RANK), const),
                  pl.BlockSpec((MLA_Q_RANK, MLA_HEADS * QK_PAD), const),
                  pl.BlockSpec((MLA_KV_RANK, MLA_HEADS * QK_PAD), const)],
        out_specs=[pl.BlockSpec((MLA_HEADS, MLA_TM, QK_PAD), lambda i: (0, i, 0)),
                   pl.BlockSpec((MLA_HEADS, MLA_TM, QK_PAD), lambda i: (0, i, 0)),
                   pl.BlockSpec((MLA_HEADS, MLA_TM, V_PAD), lambda i: (0, i, 0))],
        out_shape=[jax.ShapeDtypeStruct((MLA_HEADS, T_ALL, QK_PAD), BF16),
                   jax.ShapeDtypeStruct((MLA_HEADS, T_ALL, QK_PAD), BF16),
                   jax.ShapeDtypeStruct((MLA_HEADS, T_ALL, V_PAD), BF16)],
        compiler_params=_cparams("parallel"),
        name="mla_projection",
    )(p, p, p, c2, s2, qnw, kvnw, wq, wkv)


ATT_TQ = 512
ATT_TK = 512


def _attn_chunk(q, k, v, m, acc):
    s = lax.dot_general(q, k, (((1,), (1,)), ((), ())), preferred_element_type=F32)
    m_new = jnp.maximum(m, jnp.max(s, axis=-1, keepdims=True))
    alpha = jnp.exp(m - m_new)
    p = jnp.exp(s - m_new).astype(BF16)
    return m_new, alpha * acc + _dot(p, v)


def _attn_finish(acc, o_ref):
    o_ref[...] = acc[:, 0:MLA_V] / acc[:, MLA_V:MLA_V + 1]


def _attention_latent_kernel(q_ref, k_ref, v_ref, o_ref):
    q = q_ref[0]

    def body(c, carry):
        start = pl.multiple_of(c * ATT_TK, ATT_TK)
        return _attn_chunk(q, k_ref[0, pl.ds(start, ATT_TK), :], v_ref[0, pl.ds(start, ATT_TK), :], *carry)

    init = (jnp.full((ATT_TQ, 1), -1e30, F32), jnp.zeros((ATT_TQ, V_PAD), F32))
    m, acc = lax.fori_loop(0, SEQ // ATT_TK, body, init, unroll=2)
    m, acc = _attn_chunk(q, k_ref[0, SEQ:T_ALL, :], v_ref[0, SEQ:T_ALL, :], m, acc)
    _attn_finish(acc, o_ref)


def _attention_context_kernel(q_ref, k_ref, v_ref, o_ref):
    init = (jnp.full((CTX_LEN, 1), -1e30, F32), jnp.zeros((CTX_LEN, V_PAD), F32))
    m, acc = _attn_chunk(q_ref[0], k_ref[0], v_ref[0], *init)
    _attn_finish(acc, o_ref)


def attention(q, k, v):
    y_l = pl.pallas_call(
        _attention_latent_kernel,
        grid=(MLA_HEADS, SEQ // ATT_TQ),
        in_specs=[pl.BlockSpec((1, ATT_TQ, QK_PAD), lambda h, i: (h, i, 0)),
                  pl.BlockSpec((1, T_ALL, QK_PAD), lambda h, i: (h, 0, 0)),
                  pl.BlockSpec((1, T_ALL, V_PAD), lambda h, i: (h, 0, 0))],
        out_specs=pl.BlockSpec((ATT_TQ, MLA_V), lambda h, i: (i, h)),
        out_shape=jax.ShapeDtypeStruct((SEQ, MLA_WIDTH), F32),
        compiler_params=_cparams("parallel", "parallel"),
        name="attention_latent",
    )(q, k, v)
    cblk = SEQ // CTX_LEN
    y_c = pl.pallas_call(
        _attention_context_kernel,
        grid=(MLA_HEADS,),
        in_specs=[pl.BlockSpec((1, CTX_LEN, QK_PAD), lambda h: (h, cblk, 0)),
                  pl.BlockSpec((1, CTX_LEN, QK_PAD), lambda h: (h, cblk, 0)),
                  pl.BlockSpec((1, CTX_LEN, V_PAD), lambda h: (h, cblk, 0))],
        out_specs=pl.BlockSpec((CTX_LEN, MLA_V), lambda h: (0, h)),
        out_shape=jax.ShapeDtypeStruct((CTX_LEN, MLA_WIDTH), F32),
        compiler_params=_cparams("parallel"),
        name="attention_context",
    )(q, k, v)
    return jnp.concatenate([y_l, y_c], axis=0)


OUT_TM = 384


def _outproj_kernel(ys_ref, ym_ref, yh_ref, x_ref, mod_ref, nw_ref, w_ref, r_ref, x1_ref, h2_ref, aff_ref):
    y = _dot(ys_ref[...].astype(BF16), w_ref[0:SSD_WIDTH, :])
    y += _dot(ym_ref[...].astype(BF16), w_ref[SSD_WIDTH:SSD_WIDTH + MLA_WIDTH, :])
    y += _dot(yh_ref[...].astype(BF16), w_ref[SSD_WIDTH + MLA_WIDTH:D_MODEL, :])
    row0 = pl.program_id(0) * OUT_TM
    is_ctx = (row0 + lax.broadcasted_iota(jnp.int32, (OUT_TM, 1), 0)) >= SEQ
    gate = jnp.where(is_ctx, mod_ref[1:2, 2 * D_MODEL:3 * D_MODEL], mod_ref[0:1, 2 * D_MODEL:3 * D_MODEL])
    x1 = x_ref[...] + gate * y
    x1_ref[...] = x1
    h2 = _norm_mod(x1, nw_ref[...], mod_ref, row0, 3, 4)
    h2_ref[...] = h2
    logits = _dot_hp(h2, r_ref[...])
    lane = lax.broadcasted_iota(jnp.int32, logits.shape, 1)
    logits = jnp.where(lane < N_EXPERTS, logits, -1e30)
    e = jnp.exp(logits - jnp.max(logits, axis=-1, keepdims=True))
    aff_ref[...] = e / jnp.sum(e, axis=-1, keepdims=True)


def out_projection(y_ssd, y_mla, y_hy, x, mod, nw, w, router):
    rows = lambda i: (i, 0)
    const = lambda i: (0, 0)
    return pl.pallas_call(
        _outproj_kernel,
        grid=(T_ALL // OUT_TM,),
        in_specs=[pl.BlockSpec((OUT_TM, SSD_WIDTH), rows),
                  pl.BlockSpec((OUT_TM, MLA_WIDTH), rows),
                  pl.BlockSpec((OUT_TM, HY_WIDTH), rows),
                  pl.BlockSpec((OUT_TM, D_MODEL), rows),
                  pl.BlockSpec((8, N_MOD * D_MODEL), const),
                  pl.BlockSpec((1, D_MODEL), const),
                  pl.BlockSpec((D_MODEL, D_MODEL), const),
                  pl.BlockSpec((D_MODEL, 128), const)],
        out_specs=[pl.BlockSpec((OUT_TM, D_MODEL), rows),
                   pl.BlockSpec((OUT_TM, D_MODEL), rows),
                   pl.BlockSpec((OUT_TM, 128), rows)],
        out_shape=[jax.ShapeDtypeStruct((T_ALL, D_MODEL), F32),
                   jax.ShapeDtypeStruct((T_ALL, D_MODEL), F32),
                   jax.ShapeDtypeStruct((T_ALL, 128), F32)],
        compiler_params=_cparams("parallel"),
        name="out_projection",
    )(y_ssd, y_mla, y_hy, x, mod, nw, w, router)


def _final_norm_kernel(x_ref, w_ref, o_ref):
    x = x_ref[...]
    o_ref[...] = x * lax.rsqrt(jnp.mean(x * x, axis=-1, keepdims=True) + EPS) * w_ref[...]


def final_norm(x, w):
    tm = 512
    return pl.pallas_call(
        _final_norm_kernel,
        grid=(SEQ // tm,),
        in_specs=[pl.BlockSpec((tm, D_MODEL), lambda i: (i, 0)), pl.BlockSpec((1, D_MODEL), lambda i: (0, 0))],
        out_specs=pl.BlockSpec((tm, D_MODEL), lambda i: (i, 0)),
        out_shape=jax.ShapeDtypeStruct((SEQ, D_MODEL), F32),
        compiler_params=_cparams("parallel"),
        name="final_norm",
    )(x, w)


CONV_TM = 256
CONV_HALO = 8


def _conv_kernel(main_ref, prev_ref, next_ref, w_ref, b_ref, o_ref, ext_ref, *, taps, act):
    i = pl.program_id(0)
    n_lat = SEQ // CONV_TM
    has_prev = jnp.logical_and(i != 0, i != n_lat)
    has_next = jnp.logical_and(i != n_lat - 1, i != n_lat)
    ext_ref[0:CONV_HALO, :] = jnp.where(has_prev, prev_ref[...], 0.0)
    ext_ref[CONV_HALO:CONV_HALO + CONV_TM, :] = main_ref[...]
    ext_ref[CONV_HALO + CONV_TM:2 * CONV_HALO + CONV_TM, :] = jnp.where(has_next, next_ref[...], 0.0)
    first = CONV_HALO - taps // 2
    acc = w_ref[0:1, :] * ext_ref[first:first + CONV_TM, :] + b_ref[...]
    for k in range(1, taps):
        acc = acc + w_ref[k:k + 1, :] * ext_ref[first + k:first + k + CONV_TM, :]
    o_ref[...] = _silu(acc) if act else acc


def dwconv(p, col_block, width, w, b, act):
    taps = w.shape[0]
    r = CONV_TM // CONV_HALO
    last = T_ALL // CONV_HALO - 1
    return pl.pallas_call(
        functools.partial(_conv_kernel, taps=taps, act=act),
        grid=(T_ALL // CONV_TM,),
        in_specs=[pl.BlockSpec((CONV_TM, width), lambda i: (i, col_block)),
                  pl.BlockSpec((CONV_HALO, width), lambda i: (jnp.maximum(i * r - 1, 0), col_block)),
                  pl.BlockSpec((CONV_HALO, width), lambda i: (jnp.minimum((i + 1) * r, last), col_block)),
                  pl.BlockSpec((taps, width), lambda i: (0, 0)),
                  pl.BlockSpec((1, width), lambda i: (0, 0))],
        out_specs=pl.BlockSpec((CONV_TM, width), lambda i: (i, 0)),
        out_shape=jax.ShapeDtypeStruct((T_ALL, width), F32),
        scratch_shapes=[pltpu.VMEM((CONV_TM + 2 * CONV_HALO, width), F32)],
        compiler_params=_cparams("parallel"),
        name="dwconv",
    )(p, p, p, w, b)


N_CHUNK = T_ALL // SSD_CHUNK
N_CTX_CHUNK = CTX_LEN // SSD_CHUNK
SSD_PAIRS = SSD_HEADS // 2


def _ssd_kernel(*refs, reverse):
    if reverse:
        (xbc_ref, dt_ref, bias_ref, alog_ref, tri_ref, yf_ref, z_ref, dskip_ref, nw_ref, y_ref, s_ref) = refs
    else:
        (xbc_ref, dt_ref, bias_ref, alog_ref, tri_ref, y_ref, s_ref) = refs
    q = SSD_CHUNK

    @pl.when(pl.program_id(0) == 0)
    def _():
        s_ref[...] = jnp.zeros_like(s_ref)

    dtraw = dt_ref[...]
    if reverse:
        dtraw = pltpu.roll(dtraw, 128 - SSD_HEADS, 1)
    v = dtraw + bias_ref[...]
    dt = jnp.maximum(v, 0.0) + jnp.log(1.0 + jnp.exp(-jnp.abs(v)))
    da = dt * (-jnp.exp(alog_ref[...]))
    d1, d2, d3 = _split3(da)
    tri = tri_ref[...]
    acum = _dot(tri, d1) + _dot(tri, d2) + _dot(tri, d3)
    acum_t = acum.T
    dt_t = dt.T
    last = 0 if reverse else q - 1
    ii = lax.broadcasted_iota(jnp.int32, (q, q), 0)
    jj = lax.broadcasted_iota(jnp.int32, (q, q), 1)
    mask = (jj >= ii) if reverse else (jj <= ii)
    left = lax.broadcasted_iota(jnp.int32, (q, 128), 1) < SSD_HEAD_DIM
    left_row = lax.broadcasted_iota(jnp.int32, (1, 128), 1) < SSD_HEAD_DIM

    for g in range(SSD_GROUPS):
        bg = xbc_ref[:, SSD_WIDTH + g * SSD_STATE:SSD_WIDTH + (g + 1) * SSD_STATE]
        cg = xbc_ref[:, SSD_WIDTH + (SSD_GROUPS + g) * SSD_STATE:SSD_WIDTH + (SSD_GROUPS + g + 1) * SSD_STATE]
        bgb = bg.astype(BF16)
        cgb = cg.astype(BF16)
        cb = lax.dot_general(cgb, bgb, (((1,), (1,)), ((), ())), preferred_element_type=F32)
        bg_t = bg.T.astype(BF16)
        for pr in range(g * SSD_PAIRS // SSD_GROUPS, (g + 1) * SSD_PAIRS // SSD_GROUPS):
            sl = slice(pr * 128, (pr + 1) * 128)
            xp = xbc_ref[:, sl]
            xpb = xp.astype(BF16)
            heads = (2 * pr, 2 * pr + 1)

            def scores(h):
                seg = acum[:, h:h + 1] - acum_t[h:h + 1, :]
                decay = jnp.exp(jnp.where(mask, seg, -1e30))
                return (cb * decay * dt_t[h:h + 1, :]).astype(BF16)

            y_diag = jnp.where(left, _dot(scores(heads[0]), xpb), _dot(scores(heads[1]), xpb))
            col = [acum[:, h:h + 1] for h in heads]
            tot = [acum_t[h:h + 1, last:last + 1] for h in heads]
            sp = s_ref[:, sl]
            y_off = _dot(cgb, sp.astype(BF16)) * jnp.where(left, jnp.exp(col[0]), jnp.exp(col[1]))
            y_ref[:, sl] = y_diag + y_off
            wgt = jnp.where(left, jnp.exp(tot[0] - col[0]) * dt[:, heads[0]:heads[0] + 1],
                            jnp.exp(tot[1] - col[1]) * dt[:, heads[1]:heads[1] + 1])
            dec = jnp.where(left_row, jnp.exp(tot[0]), jnp.exp(tot[1]))
            s_ref[:, sl] = sp * dec + _dot(bg_t, (xp * wgt).astype(BF16))

    if reverse:
        y = (y_ref[...] + yf_ref[...] + dskip_ref[...] * xbc_ref[:, 0:SSD_WIDTH]) * _silu(z_ref[...])
        y_ref[...] = y * lax.rsqrt(jnp.mean(y * y, axis=-1, keepdims=True) + EPS) * nw_ref[...]


def ssd_scan(xbc, p, bias, alog, tri, reverse, yf=None, dskip=None, nw=None):
    if reverse:
        order = lambda s: N_CHUNK - 1 - s
    else:
        order = lambda s: jnp.where(s < N_CTX_CHUNK, N_CHUNK - N_CTX_CHUNK + s, s - N_CTX_CHUNK)
    const = lambda s: (0, 0)
    in_specs = [pl.BlockSpec((SSD_CHUNK, SSD_XBC), lambda s: (order(s), 0)),
                pl.BlockSpec((SSD_CHUNK, 128), lambda s: (order(s), P_DT // 128)),
                pl.BlockSpec((1, 128), const),
                pl.BlockSpec((1, 128), const),
                pl.BlockSpec((SSD_CHUNK, SSD_CHUNK), const)]
    args = [xbc, p, bias, alog, tri]
    if reverse:
        in_specs += [pl.BlockSpec((SSD_CHUNK, SSD_WIDTH), lambda s: (order(s), 0)),
                     pl.BlockSpec((SSD_CHUNK, SSD_WIDTH), lambda s: (order(s), P_Z // SSD_WIDTH)),
                     pl.BlockSpec((1, SSD_WIDTH), const),
                     pl.BlockSpec((1, SSD_WIDTH), const)]
        args += [yf, p, dskip, nw]
    return pl.pallas_call(
        functools.partial(_ssd_kernel, reverse=reverse),
        grid=(N_CHUNK,),
        in_specs=in_specs,
        out_specs=pl.BlockSpec((SSD_CHUNK, SSD_WIDTH), lambda s: (order(s), 0)),
        out_shape=jax.ShapeDtypeStruct((T_ALL, SSD_WIDTH), F32),
        scratch_shapes=[pltpu.VMEM((SSD_STATE, SSD_WIDTH), F32)],
        compiler_params=_cparams("arbitrary"),
        name="ssd_scan_bwd" if reverse else "ssd_scan_fwd",
    )(*args)


ROUTE_TB = 256
WS_ROWS = 72


def _route_kernel(aff_ref, lstrict_ref, pos_ref, ws_ref):
    lstrict = lstrict_ref[...]
    ws_ref[...] = jnp.zeros_like(ws_ref)

    def stream(row0, nrows, k, slot_base, ws_row0):
        nblk = nrows // ROUTE_TB

        def bits(b):
            r0 = pl.multiple_of(row0 + b * ROUTE_TB, ROUTE_TB)
            return r0, pltpu.bitcast(aff_ref[pl.ds(r0, ROUTE_TB), :], jnp.int32)

        def count(pred):
            def body(b, acc):
                return acc + jnp.sum(jnp.where(pred(bits(b)[1]), 1.0, 0.0), axis=0, keepdims=True)
            return lax.fori_loop(0, nblk, body, jnp.zeros((1, 128), F32))

        def search(it, thr):
            cand = thr | jnp.left_shift(jnp.int32(1), 30 - it)
            return jnp.where(count(lambda v: v >= cand) >= k, cand, thr)

        thr = lax.fori_loop(0, 31, search, jnp.zeros((1, 128), jnp.int32))
        need = k - count(lambda v: v > thr)

        def place(b, carry):
            ties_before, picked_before = carry
            r0, v = bits(b)
            gt = v > thr
            eq = v == thr
            eqf = jnp.where(eq, 1.0, 0.0)
            tie_rank = _dot(lstrict, eqf.astype(BF16)) + ties_before
            sel = jnp.logical_or(gt, jnp.logical_and(eq, tie_rank < need))
            self_ = jnp.where(sel, 1.0, 0.0)
            rank = _dot(lstrict, self_.astype(BF16)) + picked_before
            pos_ref[pl.ds(r0, ROUTE_TB), :] = jnp.where(sel, rank + slot_base, -1.0).astype(jnp.int32)
            w = ws_row0 + 2 * b
            ws_ref[pl.ds(w, 1), :] = (picked_before + slot_base).astype(jnp.int32)
            ws_ref[pl.ds(w + 1, 1), :] = (rank[128:129, :] + slot_base).astype(jnp.int32)
            return (ties_before + jnp.sum(eqf, axis=0, keepdims=True),
                    picked_before + jnp.sum(self_, axis=0, keepdims=True))

        zero = jnp.zeros((1, 128), F32)
        lax.fori_loop(0, nblk, place, (zero, zero))

    stream(0, SEQ, CAP_L, 0.0, 0)
    stream(SEQ, CTX_LEN, CAP_C, float(CAP_L), SEQ // 128)


def route(aff, lstrict):
    return pl.pallas_call(
        _route_kernel,
        out_shape=[jax.ShapeDtypeStruct((T_ALL, 128), jnp.int32),
                   jax.ShapeDtypeStruct((WS_ROWS, 128), jnp.int32)],
        compiler_params=pltpu.CompilerParams(vmem_limit_bytes=VMEM_LIMIT),
        name="route",
    )(aff, lstrict)


ACC_ROWS = 1280


def _compact_kernel(ws_ref, pos_ref, tv_ref, acc_ref):
    acc_ref[...] = jnp.zeros_like(acc_ref)
    tv = tv_ref[...]
    lane = lax.broadcasted_iota(jnp.int32, (1, 128), 1)
    srow = lax.broadcasted_iota(jnp.int32, (256, 1), 0)

    def body(tb, carry):
        r0 = pl.multiple_of(tb * 128, 128)
        slot_of_token = pos_ref[pl.ds(r0, 128), :].astype(F32).T
        scale = jnp.where(lane == 0, 1.0, jnp.where(lane == 1, jnp.asarray(tb * 128, F32), 0.0))
        for e in range(N_EXPERTS):
            base = pl.multiple_of(jnp.minimum(ws_ref[tb, e] & -128, ACC_ROWS - 256), 128)
            onehot = jnp.where(slot_of_token[e:e + 1, :] == (srow + base).astype(F32), 1.0, 0.0)
            found = _dot(onehot.astype(BF16), tv)
            acc_ref[e, pl.ds(base, 256), :] += found * scale
        return carry

    lax.fori_loop(0, N_TB, body, 0)


def compact(ws, pos, tv):
    return pl.pallas_call(
        _compact_kernel,
        grid_spec=pltpu.PrefetchScalarGridSpec(
            num_scalar_prefetch=1,
            grid=(1,),
            in_specs=[pl.BlockSpec((T_ALL, 128), lambda i, ws: (0, 0)),
                      pl.BlockSpec((128, 128), lambda i, ws: (0, 0))],
            out_specs=pl.BlockSpec((N_EXPERTS, ACC_ROWS, 128), lambda i, ws: (0, 0, 0))),
        out_shape=jax.ShapeDtypeStruct((N_EXPERTS, ACC_ROWS, 128), F32),
        compiler_params=_cparams("arbitrary"),
        name="compact",
    )(ws, pos, tv)


FFN_TF = 256
GATHER_ROWS = 128
N_GATHER = SLOTS // GATHER_ROWS


def _ffn_kernel(idx_ref, h_ref, w1_ref, w3_ref, w2_ref, y_ref, x_ref, acc_ref, gbuf_ref, sem_ref):
    e = pl.program_id(0)
    f = pl.program_id(1)

    def row_copy(chunk, slot, r, token):
        return pltpu.make_async_copy(h_ref.at[pl.ds(token, 1), :], gbuf_ref.at[slot, pl.ds(r, 1), :],
                                     sem_ref.at[slot])

    def issue(chunk, slot):
        def body(r, carry):
            row_copy(chunk, slot, r, idx_ref[e, chunk * GATHER_ROWS + r]).start()
            return carry
        lax.fori_loop(0, GATHER_ROWS, body, 0)

    def drain(chunk, slot):
        def body(r, carry):
            row_copy(chunk, slot, r, 0).wait()
            return carry
        lax.fori_loop(0, GATHER_ROWS, body, 0)
        x_ref[chunk * GATHER_ROWS:(chunk + 1) * GATHER_ROWS, :] = gbuf_ref[slot].astype(BF16)

    @pl.when(f == 0)
    def _():
        issue(0, 0)
        for chunk in range(N_GATHER):
            if chunk + 1 < N_GATHER:
                issue(chunk + 1, (chunk + 1) % 2)
            drain(chunk, chunk % 2)

    x = x_ref[...]
    a = _dot(x, w1_ref[...].astype(BF16))
    b = _dot(x, w3_ref[...].astype(BF16))
    hid = (_silu(a) * b).astype(BF16)
    part = _dot(hid, w2_ref[...].astype(BF16))

    @pl.when(f == 0)
    def _():
        acc_ref[...] = part

    @pl.when(f > 0)
    def _():
        acc_ref[...] += part

    @pl.when(f == EXPERT_FF // FFN_TF - 1)
    def _():
        y_ref[...] = acc_ref[...].astype(BF16)


def expert_ffn(idx, h2, w1, w3, w2, layer):
    return pl.pallas_call(
        _ffn_kernel,
        grid_spec=pltpu.PrefetchScalarGridSpec(
            num_scalar_prefetch=1,
            grid=(N_EXPERTS, EXPERT_FF // FFN_TF),
            in_specs=[pl.BlockSpec(memory_space=pl.ANY),
                      pl.BlockSpec((None, None, D_MODEL, FFN_TF), lambda e, f, idx: (layer, e, 0, f)),
                      pl.BlockSpec((None, None, D_MODEL, FFN_TF), lambda e, f, idx: (layer, e, 0, f)),
                      pl.BlockSpec((None, None, FFN_TF, D_MODEL), lambda e, f, idx: (layer, e, f, 0))],
            out_specs=pl.BlockSpec((None, SLOTS, D_MODEL), lambda e, f, idx: (e, 0, 0)),
            scratch_shapes=[pltpu.VMEM((SLOTS, D_MODEL), BF16),
                            pltpu.VMEM((SLOTS, D_MODEL), F32),
                            pltpu.VMEM((2, GATHER_ROWS, D_MODEL), F32),
                            pltpu.SemaphoreType.DMA((2,))]),
        out_shape=jax.ShapeDtypeStruct((N_EXPERTS, SLOTS, D_MODEL), BF16),
        compiler_params=_cparams("arbitrary", "arbitrary"),
        name="expert_ffn",
    )(idx, h2, w1, w3, w2)


CMB_TD = 512
CMB_WIN = 256


def _combine_kernel(ws_ref, y_ref, pos_ref, aff_ref, x1_ref, g_ref, o_ref):
    tb = pl.program_id(1)
    pos = pos_ref[...]
    aff = aff_ref[...]
    lane = lax.broadcasted_iota(jnp.int32, (1, CMB_WIN), 1)
    acc = jnp.zeros((128, CMB_TD), F32)
    for e in range(N_EXPERTS):
        base = pl.multiple_of(jnp.minimum(ws_ref[tb, e] & -128, SLOTS - CMB_WIN), 128)
        onehot = jnp.where(pos[:, e:e + 1] == lane + base, 1.0, 0.0).astype(BF16)
        acc = acc + aff[:, e:e + 1] * _dot(onehot, y_ref[e, pl.ds(base, CMB_WIN), :])
    g = jnp.where(tb >= SEQ // 128, g_ref[1:2, :], g_ref[0:1, :])
    o_ref[...] = x1_ref[...] + g * acc


def combine(ws, y, pos, aff, x1, mod):
    nd = D_MODEL // CMB_TD
    return pl.pallas_call(
        _combine_kernel,
        grid_spec=pltpu.PrefetchScalarGridSpec(
            num_scalar_prefetch=1,
            grid=(nd, N_TB),
            in_specs=[pl.BlockSpec((N_EXPERTS, SLOTS, CMB_TD), lambda d, t, ws: (0, 0, d)),
                      pl.BlockSpec((128, 128), lambda d, t, ws: (t, 0)),
                      pl.BlockSpec((128, 128), lambda d, t, ws: (t, 0)),
                      pl.BlockSpec((128, CMB_TD), lambda d, t, ws: (t, d)),
                      pl.BlockSpec((8, CMB_TD), lambda d, t, ws: (0, 5 * nd + d))],
            out_specs=pl.BlockSpec((128, CMB_TD), lambda d, t, ws: (t, d))),
        out_shape=jax.ShapeDtypeStruct((T_ALL, D_MODEL), F32),
        compiler_params=_cparams("arbitrary", "arbitrary"),
        name="combine",
    )(ws, y, pos, aff, x1, mod)


def rms_norm(x, w):
    xf = x.astype(jnp.float32)
    xf = xf * lax.rsqrt(jnp.mean(xf * xf, axis=-1, keepdims=True) + EPS)
    return (xf * w.astype(jnp.float32)).astype(x.dtype)


def centred_dwconv(x, w, b):
    k = w.shape[0]
    y = lax.conv_general_dilated(x, w[:, None, :].astype(x.dtype), window_strides=(1,),
                                 padding=[(k // 2, k // 2)], dimension_numbers=('NWC', 'WIO', 'NWC'),
                                 feature_group_count=x.shape[-1])
    return y + b


def ssd_chunked(xs, dt, a, bm, cm, h0, need_y):
    b, L, H, P = xs.shape
    nc = L // SSD_CHUNK
    shp = (b, nc, SSD_CHUNK, H)
    xc = xs.reshape(*shp, P)
    bc = bm.reshape(*shp, SSD_STATE)
    cc = cm.reshape(*shp, SSD_STATE)
    dtc = dt.reshape(shp)
    acum = jnp.cumsum(dtc * a, axis=2)
    to_end = jnp.exp(acum[:, :, -1:] - acum)
    states = jnp.einsum('bcjhn,bcjh,bcjhp->bchpn', bc, to_end * dtc, xc)
    chunk_decay = jnp.exp(acum[:, :, -1])

    def step(h, inp):
        s, d = inp
        return h * d[:, :, None, None] + s, h

    h_final, h_in = lax.scan(step, h0, (jnp.moveaxis(states, 1, 0), jnp.moveaxis(chunk_decay, 1, 0)))
    if not need_y:
        return None, h_final
    h_in = jnp.moveaxis(h_in, 0, 1)
    seg = acum[:, :, :, None, :] - acum[:, :, None, :, :]
    lower = jnp.tril(jnp.ones((SSD_CHUNK, SSD_CHUNK), dtype=bool))[:, :, None]
    decay = jnp.exp(jnp.where(lower, seg, -jnp.inf))
    scores = jnp.einsum('bcihn,bcjhn->bcijh', cc, bc) * decay
    y_diag = jnp.einsum('bcijh,bcjh,bcjhp->bcihp', scores, dtc, xc)
    y_off = jnp.einsum('bcihn,bchpn,bcih->bcihp', cc, h_in, jnp.exp(acum))
    return (y_diag + y_off).reshape(b, L, H, P), h_final


def ssd_branch(z, xbc, dt_raw, conv_w, conv_b, dt_bias, a_log, d_skip, norm_w, h0_f, h0_b, need_out):
    b, L, _ = xbc.shape
    xbc = jax.nn.silu(centred_dwconv(xbc, conv_w, conv_b))
    xs, bm, cm = jnp.split(xbc, [SSD_WIDTH, SSD_WIDTH + SSD_GROUPS * SSD_STATE], axis=-1)
    xs = xs.reshape(b, L, SSD_HEADS, SSD_HEAD_DIM)
    rep = SSD_HEADS // SSD_GROUPS
    bm = jnp.repeat(bm.reshape(b, L, SSD_GROUPS, SSD_STATE), rep, axis=2)
    cm = jnp.repeat(cm.reshape(b, L, SSD_GROUPS, SSD_STATE), rep, axis=2)
    dt = jax.nn.softplus(dt_raw.reshape(b, L, 2, SSD_HEADS) + dt_bias)
    a = -jnp.exp(a_log)
    y_f, h_f = ssd_chunked(xs, dt[:, :, 0], a[0], bm, cm, h0_f, need_out)
    y_b, h_b = ssd_chunked(jnp.flip(xs, 1), jnp.flip(dt[:, :, 1], 1), a[1], jnp.flip(bm, 1),
                           jnp.flip(cm, 1), h0_b, need_out)
    if not need_out:
        return None, h_f, h_b
    y = y_f + jnp.flip(y_b, 1) + d_skip[:, None] * xs
    y = rms_norm(y.reshape(b, L, SSD_WIDTH) * jax.nn.silu(z), norm_w)
    return y, h_f, h_b


def hyena_filters(L, w1, b1, w2, b2, w3, b3, w_out, freq):
    f32 = jnp.float32
    t = jnp.linspace(0.0, 1.0, L, dtype=f32)[:, None]
    bands = jnp.linspace(1e-4, HY_BANDS - 1, HY_BANDS, dtype=f32)
    ang = 2 * math.pi * jnp.arange(L, dtype=f32)[:, None] * bands / L
    feats = jnp.concatenate([t, jnp.cos(ang), -jnp.sin(ang)], axis=-1)
    fr = freq.astype(f32)
    hdn = jnp.sin(fr * (feats @ w1.astype(f32) + b1.astype(f32)))
    hdn = jnp.sin(fr * (hdn @ w2.astype(f32) + b2.astype(f32)))
    hdn = jnp.sin(fr * (hdn @ w3.astype(f32) + b3.astype(f32)))
    h = (hdn @ w_out.astype(f32)).reshape(L, HY_ORDER, 2, HY_WIDTH)
    deltas = jnp.abs(jnp.linspace(math.log(HY_TARGET) / HY_SLOW_DECAY_PCT,
                                  math.log(HY_TARGET) / HY_FAST_DECAY_PCT, HY_WIDTH, dtype=f32))
    h = h * jnp.exp(-t * deltas)[:, None, None, :]
    return h / jnp.sum(jnp.abs(h), axis=(0, 2), keepdims=True)


def bidir_fftconv(u, h_fwd, h_bwd):
    L = u.shape[1]
    k = jnp.concatenate([h_fwd, jnp.zeros_like(h_fwd[:1]), h_bwd[:0:-1]], axis=0)
    kf = jnp.fft.rfft(k, n=2 * L, axis=0)
    uf = jnp.fft.rfft(u.astype(jnp.float32), n=2 * L, axis=1)
    return jnp.fft.irfft(uf * kf, n=2 * L, axis=1)[:, :L].astype(u.dtype)


def hyena_branch(u, short_w, short_b, w1, b1, w2, b2, w3, b3, w_out, freq, bias):
    L = u.shape[1]
    u = centred_dwconv(u, short_w, short_b)
    v, x1, x2 = jnp.split(u, HY_ORDER + 1, axis=-1)
    h = hyena_filters(L, w1, b1, w2, b2, w3, b3, w_out, freq)
    z = v
    for o, gate in enumerate((x1, x2)):
        z = gate * (bidir_fftconv(z, h[:, o, 0], h[:, o, 1]) + bias[o] * z)
    return z


def ec_moe(h, router, w1, w3, w2):
    b, L, d = h.shape
    cap = CAPACITY_FACTOR * L // N_EXPERTS
    logits = jnp.einsum('bld,de->ble', h, router, preferred_element_type=jnp.float32)
    aff = jax.nn.softmax(logits, axis=-1)
    gates, idx = lax.top_k(jnp.swapaxes(aff, 1, 2), cap)
    xs = jax.vmap(lambda hb, ib: hb[ib])(h, idx)
    hid = jax.nn.silu(jnp.einsum('becd,edf->becf', xs, w1)) * jnp.einsum('becd,edf->becf', xs, w3)
    out = jnp.einsum('becf,efd->becd', hid, w2) * gates[..., None].astype(h.dtype)
    return jax.vmap(lambda ob, ib: jnp.zeros((L, d), ob.dtype).at[ib.reshape(-1)].add(ob.reshape(-1, d)))(out, idx)


def _rope_tables():
    rows = SEQ // GRID_W
    r, col = jnp.meshgrid(jnp.arange(rows), jnp.arange(GRID_W), indexing='ij')
    per_axis = MLA_ROPE // 2
    inv = ROPE_THETA ** (-jnp.arange(0, per_axis, 2, dtype=F32) / per_axis)
    ang = jnp.concatenate([r.reshape(-1, 1) * inv, col.reshape(-1, 1) * inv], axis=-1)
    cos = jnp.concatenate([jnp.cos(ang), jnp.ones((CTX_LEN, per_axis), F32)], axis=0)
    sin = jnp.concatenate([jnp.sin(ang), jnp.zeros((CTX_LEN, per_axis), F32)], axis=0)
    z = jnp.zeros_like(cos)
    return jnp.concatenate([cos, z, cos, z], axis=-1), jnp.concatenate([-sin, z, sin, z], axis=-1)


def _permute_w_in(w_in):
    z, xbc, dt, cq, ckv, kr, hy = jnp.split(w_in, np.cumsum(
        (SSD_WIDTH, SSD_XBC, 2 * SSD_HEADS, MLA_Q_RANK, MLA_KV_RANK, MLA_ROPE)).tolist(), axis=-1)
    zeros = lambda n: jnp.zeros(w_in.shape[:-1] + (n,), w_in.dtype)
    return jnp.concatenate([hy, z, ckv, xbc, kr[..., :32], zeros(32), kr[..., 32:], zeros(32),
                            dt, zeros(128 - 2 * SSD_HEADS), cq], axis=-1).astype(BF16)


def _pad_w_uq(w_uq):
    w = w_uq.reshape(DEPTH, MLA_Q_RANK, MLA_HEADS, MLA_QK)
    zeros = jnp.zeros(w.shape[:-1] + (32,), w.dtype)
    w = jnp.concatenate([w[..., :MLA_NOPE], w[..., MLA_NOPE:MLA_NOPE + 32], zeros,
                         w[..., MLA_NOPE + 32:], zeros], axis=-1)
    return w.reshape(DEPTH, MLA_Q_RANK, MLA_HEADS * QK_PAD).astype(BF16)


def kernel(x, c, ctx, c_ctx, w_mod, b_mod, norm1_w, norm2_w, w_in, w_out, ssd_conv_w, ssd_conv_b,
           ssd_dt_bias, ssd_a_log, ssd_d, ssd_norm_w, mla_q_norm_w, mla_w_uq, mla_kv_norm_w, mla_w_ukv,
           hy_short_w, hy_short_b, hy_w1, hy_b1, hy_w2, hy_b2, hy_w3, hy_b3, hy_w_out, hy_freq, hy_bias,
           moe_router, moe_w1, moe_w3, moe_w2, final_norm_w):
    xj = jnp.concatenate([x[0], ctx[0]], axis=0)
    cvec = jnp.concatenate([c, c_ctx[None, :], jnp.zeros((6, D_MODEL), F32)], axis=0)
    mods = modulation(cvec, w_mod, b_mod)
    c2, s2 = _rope_tables()
    w_in_p = _permute_w_in(w_in)
    w_uq_p = _pad_w_uq(mla_w_uq)
    w_ukv_b = mla_w_ukv.astype(BF16)
    w_out_b = w_out.astype(BF16)
    router_p = jnp.pad(moe_router, ((0, 0), (0, 0), (0, 128 - N_EXPERTS)))
    lane_pad = lambda a: jnp.pad(a, ((0, 0), (0, 0), (0, 128 - SSD_HEADS)))[:, :, None, :]
    dt_bias = lane_pad(ssd_dt_bias)
    a_log = lane_pad(ssd_a_log)
    d_skip = jnp.repeat(ssd_d, SSD_HEAD_DIM, axis=-1)[:, None, :]
    tri_lower = jnp.asarray(np.tril(np.ones((SSD_CHUNK, SSD_CHUNK), np.float32)), BF16)
    tri_upper = jnp.asarray(np.triu(np.ones((SSD_CHUNK, SSD_CHUNK), np.float32)), BF16)
    lstrict = jnp.asarray(np.tril(np.ones((ROUTE_TB, ROUTE_TB), np.float32), -1), BF16)
    tv = np.zeros((128, 128), np.float32)
    tv[:, 0] = np.arange(128)
    tv[:, 1] = 1.0
    tvals = jnp.asarray(tv, BF16)

    for i in range(DEPTH):
        mod = mods[i]
        p = in_projection(xj, mod, norm1_w[i][None, :], w_in_p[i])
        q, k, v = mla_projection(p, c2, s2, mla_q_norm_w[i][None, :], mla_kv_norm_w[i][None, :],
                                 w_uq_p[i], w_ukv_b[i])
        y_mla = attention(q, k, v)

        xbc = dwconv(p, P_XBC // SSD_XBC, SSD_XBC, ssd_conv_w[i], ssd_conv_b[i][None, :], act=True)
        y_f = ssd_scan(xbc, p, dt_bias[i, 0], a_log[i, 0], tri_lower, reverse=False)
        y_ssd = ssd_scan(xbc, p, dt_bias[i, 1], a_log[i, 1], tri_upper, reverse=True, yf=y_f,
                         dskip=d_skip[i], nw=ssd_norm_w[i][None, :])

        hy = p[:, P_HY:P_HY + 3 * HY_WIDTH]
        hy_p = (hy_short_w[i], hy_short_b[i], hy_w1[i], hy_b1[i], hy_w2[i], hy_b2[i],
                hy_w3[i], hy_b3[i], hy_w_out[i], hy_freq[i], hy_bias[i])
        y_hy = jnp.concatenate([hyena_branch(hy[None, :SEQ], *hy_p)[0],
                                hyena_branch(hy[None, SEQ:], *hy_p)[0]], axis=0)

        x1, h2, aff = out_projection(y_ssd, y_mla, y_hy, xj, mod, norm2_w[i][None, :], w_out_b[i], router_p[i])
        pos, ws = route(aff, lstrict)
        found = compact(ws, pos, tvals)
        idx = (found[:, :SLOTS, 0] + found[:, :SLOTS, 1]).astype(jnp.int32)
        y_moe = expert_ffn(idx, h2, moe_w1, moe_w3, moe_w2, i)
        xj = combine(ws, y_moe, pos, aff, x1, mod)
    return final_norm(xj, final_norm_w[None, :])[None]
```

```python
import functools
import math

import jax
import jax.numpy as jnp
import numpy as np
from jax import lax
from jax.experimental import pallas as pl
from jax.experimental.pallas import tpu as pltpu

D_MODEL = 2048
SEQ = 8192
DEPTH = 4
GRID_W = 64
CTX_LEN = 256
T_ALL = SEQ + CTX_LEN
EPS = 1e-6
N_MOD = 6

SSD_HEAD_DIM = 64
SSD_WIDTH = 768
SSD_HEADS = 12
SSD_GROUPS = 2
SSD_STATE = 128
SSD_CONV = 5
SSD_CHUNK = 128
SSD_XBC = SSD_WIDTH + 2 * SSD_GROUPS * SSD_STATE
MLA_NOPE = 128
MLA_ROPE = 64
MLA_V = 128
MLA_QK = MLA_NOPE + MLA_ROPE
MLA_WIDTH = 768
MLA_HEADS = 6
MLA_Q_RANK = 512
MLA_KV_RANK = 256
ROPE_THETA = 10000.0
HY_WIDTH = 512
HY_ORDER = 2
HY_SHORT = 3
HY_BANDS = 16
HY_EMB = 1 + 2 * HY_BANDS
HY_FFN = 64
HY_TARGET = 1e-2
HY_FAST_DECAY_PCT = 0.3
HY_SLOW_DECAY_PCT = 1.5
N_EXPERTS = 16
EXPERT_FF = 1024
CAPACITY_FACTOR = 2

P_HY = 0
P_Z = 1536
P_CKV = 2304
P_XBC = 2560
P_KR = 3840
P_DT = 3968
P_CQ = 4096
P_COLS = 4608
QK_PAD = 256
V_PAD = 256
ATT_SCALE = 1.0 / math.sqrt(MLA_QK)
CAP_L = CAPACITY_FACTOR * SEQ // N_EXPERTS
CAP_C = CAPACITY_FACTOR * CTX_LEN // N_EXPERTS
SLOTS = 1152
N_TB = T_ALL // 128

F32 = jnp.float32
BF16 = jnp.bfloat16
VMEM_LIMIT = 56 * 1024 * 1024


def _cparams(*sem):
    return pltpu.CompilerParams(dimension_semantics=sem, vmem_limit_bytes=VMEM_LIMIT)


def _split3(a):
    hi = a.astype(BF16)
    r = a - hi.astype(F32)
    mid = r.astype(BF16)
    lo = (r - mid.astype(F32)).astype(BF16)
    return hi, mid, lo


def _dot(a, b):
    return jnp.dot(a, b, preferred_element_type=F32)


def _dot_hp(a, b):
    a1, a2, a3 = _split3(a)
    b1, b2, b3 = _split3(b)
    return (_dot(a1, b1) + (_dot(a1, b2) + _dot(a2, b1))
            + (_dot(a2, b2) + _dot(a1, b3) + _dot(a3, b1)))


def _silu(v):
    return v * (1.0 / (1.0 + jnp.exp(-v)))


def _mod_kernel(c_ref, w_ref, b_ref, o_ref):
    s = _silu(c_ref[...])
    o_ref[0] = _dot_hp(s, w_ref[0]) + b_ref[0]


def modulation(cvec, w_mod, b_mod):
    tn = 1024
    return pl.pallas_call(
        _mod_kernel,
        grid=(DEPTH, N_MOD * D_MODEL // tn),
        in_specs=[pl.BlockSpec((8, D_MODEL), lambda l, j: (0, 0)),
                  pl.BlockSpec((1, D_MODEL, tn), lambda l, j: (l, 0, j)),
                  pl.BlockSpec((1, 1, tn), lambda l, j: (l, 0, j))],
        out_specs=pl.BlockSpec((1, 8, tn), lambda l, j: (l, 0, j)),
        out_shape=jax.ShapeDtypeStruct((DEPTH, 8, N_MOD * D_MODEL), F32),
        compiler_params=_cparams("parallel", "parallel"),
        name="modulation",
    )(cvec, w_mod, b_mod.reshape(DEPTH, 1, N_MOD * D_MODEL))


def _norm_mod(x, nw, mod_ref, row0, k_shift, k_scale):
    n = x.shape[0]
    xn = x * lax.rsqrt(jnp.mean(x * x, axis=-1, keepdims=True) + EPS) * nw
    is_ctx = (row0 + lax.broadcasted_iota(jnp.int32, (n, 1), 0)) >= SEQ
    sl_shift = slice(k_shift * D_MODEL, (k_shift + 1) * D_MODEL)
    sl_scale = slice(k_scale * D_MODEL, (k_scale + 1) * D_MODEL)
    shift = jnp.where(is_ctx, mod_ref[1:2, sl_shift], mod_ref[0:1, sl_shift])
    scale = jnp.where(is_ctx, mod_ref[1:2, sl_scale], mod_ref[0:1, sl_scale])
    return xn * (1.0 + scale) + shift


IN_TM = 768
IN_TN = 768


def _inproj_kernel(x_ref, mod_ref, nw_ref, w_ref, o_ref, h_ref):
    @pl.when(pl.program_id(1) == 0)
    def _():
        h = _norm_mod(x_ref[...], nw_ref[...], mod_ref, pl.program_id(0) * IN_TM, 0, 1)
        h_ref[...] = h.astype(BF16)

    o_ref[...] = _dot(h_ref[...], w_ref[...])


def in_projection(x, mod, nw, w):
    return pl.pallas_call(
        _inproj_kernel,
        grid=(T_ALL // IN_TM, P_COLS // IN_TN),
        in_specs=[pl.BlockSpec((IN_TM, D_MODEL), lambda i, j: (i, 0)),
                  pl.BlockSpec((8, N_MOD * D_MODEL), lambda i, j: (0, 0)),
                  pl.BlockSpec((1, D_MODEL), lambda i, j: (0, 0)),
                  pl.BlockSpec((D_MODEL, IN_TN), lambda i, j: (0, j))],
        out_specs=pl.BlockSpec((IN_TM, IN_TN), lambda i, j: (i, j)),
        out_shape=jax.ShapeDtypeStruct((T_ALL, P_COLS), F32),
        scratch_shapes=[pltpu.VMEM((IN_TM, D_MODEL), BF16)],
        compiler_params=_cparams("parallel", "arbitrary"),
        name="in_projection",
    )(x, mod, nw, w)


MLA_TM = 768


def _rope(v, c2, s2):
    return v * c2 + pltpu.roll(v, 64, 1) * s2


def _mla_proj_kernel(cq_ref, ckv_ref, kr_ref, c2_ref, s2_ref, qnw_ref, kvnw_ref, wq_ref, wkv_ref,
                     q_ref, k_ref, v_ref):
    c2 = c2_ref[...]
    s2 = s2_ref[...]
    cq = cq_ref[...]
    cqn = cq * lax.rsqrt(jnp.mean(cq * cq, axis=-1, keepdims=True) + EPS) * qnw_ref[...]
    q = _dot(cqn.astype(BF16), wq_ref[...]) * ATT_SCALE
    ckv = ckv_ref[...]
    ckvn = ckv * lax.rsqrt(jnp.mean(ckv * ckv, axis=-1, keepdims=True) + EPS) * kvnw_ref[...]
    kv = _dot(ckvn.astype(BF16), wkv_ref[...])
    kr = _rope(kr_ref[...], c2, s2).astype(BF16)
    for h in range(MLA_HEADS):
        b = h * QK_PAD
        q_ref[h, :, 0:MLA_NOPE] = q[:, b:b + MLA_NOPE].astype(BF16)
        q_ref[h, :, MLA_NOPE:QK_PAD] = _rope(q[:, b + MLA_NOPE:b + QK_PAD], c2, s2).astype(BF16)
        k_ref[h, :, 0:MLA_NOPE] = kv[:, b:b + MLA_NOPE].astype(BF16)
        k_ref[h, :, MLA_NOPE:QK_PAD] = kr
        v_ref[h, :, 0:MLA_V] = kv[:, b + MLA_NOPE:b + QK_PAD].astype(BF16)
        v_ref[h, :, MLA_V:V_PAD] = jnp.ones((MLA_TM, V_PAD - MLA_V), BF16)


def mla_projection(p, c2, s2, qnw, kvnw, wq, wkv):
    rows = lambda i: (i, 0)
    const = lambda i: (0, 0)
    return pl.pallas_call(
        _mla_proj_kernel,
        grid=(T_ALL // MLA_TM,),
        in_specs=[pl.BlockSpec((MLA_TM, MLA_Q_RANK), lambda i: (i, P_CQ // MLA_Q_RANK)),
                  pl.BlockSpec((MLA_TM, MLA_KV_RANK), lambda i: (i, P_CKV // MLA_KV_RANK)),
                  pl.BlockSpec((MLA_TM, 128), lambda i: (i, P_KR // 128)),
                  pl.BlockSpec((MLA_TM, 128), rows),
                  pl.BlockSpec((MLA_TM, 128), rows),
                  pl.BlockSpec((1, MLA_Q_RANK), const),
                  pl.BlockSpec((1, MLA_KV_RANK), const),
                  pl.BlockSpec((MLA_Q_RANK, MLA_HEADS * QK_PAD), const),
                  pl.BlockSpec((MLA_KV_RANK, MLA_HEADS * QK_PAD), const)],
        out_specs=[pl.BlockSpec((MLA_HEADS, MLA_TM, QK_PAD), lambda i: (0, i, 0)),
                   pl.BlockSpec((MLA_HEADS, MLA_TM, QK_PAD), lambda i: (0, i, 0)),
                   pl.BlockSpec((MLA_HEADS, MLA_TM, V_PAD), lambda i: (0, i, 0))],
        out_shape=[jax.ShapeDtypeStruct((MLA_HEADS, T_ALL, QK_PAD), BF16),
                   jax.ShapeDtypeStruct((MLA_HEADS, T_ALL, QK_PAD), BF16),
                   jax.ShapeDtypeStruct((MLA_HEADS, T_ALL, V_PAD), BF16)],
        compiler_params=_cparams("parallel"),
        name="mla_projection",
    )(p, p, p, c2, s2, qnw, kvnw, wq, wkv)


ATT_TQ = 512
ATT_TK = 512


def _attn_chunk(q, k, v, m, acc):
    s = lax.dot_general(q, k, (((1,), (1,)), ((), ())), preferred_element_type=F32)
    m_new = jnp.maximum(m, jnp.max(s, axis=-1, keepdims=True))
    alpha = jnp.exp(m - m_new)
    p = jnp.exp(s - m_new).astype(BF16)
    return m_new, alpha * acc + _dot(p, v)


def _attn_finish(acc, o_ref):
    o_ref[...] = acc[:, 0:MLA_V] / acc[:, MLA_V:MLA_V + 1]


def _attention_latent_kernel(q_ref, k_ref, v_ref, o_ref):
    q = q_ref[0]

    def body(c, carry):
        start = pl.multiple_of(c * ATT_TK, ATT_TK)
        return _attn_chunk(q, k_ref[0, pl.ds(start, ATT_TK), :], v_ref[0, pl.ds(start, ATT_TK), :], *carry)

    init = (jnp.full((ATT_TQ, 1), -1e30, F32), jnp.zeros((ATT_TQ, V_PAD), F32))
    m, acc = lax.fori_loop(0, SEQ // ATT_TK, body, init, unroll=2)
    m, acc = _attn_chunk(q, k_ref[0, SEQ:T_ALL, :], v_ref[0, SEQ:T_ALL, :], m, acc)
    _attn_finish(acc, o_ref)


def _attention_context_kernel(q_ref, k_ref, v_ref, o_ref):
    init = (jnp.full((CTX_LEN, 1), -1e30, F32), jnp.zeros((CTX_LEN, V_PAD), F32))
    m, acc = _attn_chunk(q_ref[0], k_ref[0], v_ref[0], *init)
    _attn_finish(acc, o_ref)


def attention(q, k, v):
    y_l = pl.pallas_call(
        _attention_latent_kernel,
        grid=(MLA_HEADS, SEQ // ATT_TQ),
        in_specs=[pl.BlockSpec((1, ATT_TQ, QK_PAD), lambda h, i: (h, i, 0)),
                  pl.BlockSpec((1, T_ALL, QK_PAD), lambda h, i: (h, 0, 0)),
                  pl.BlockSpec((1, T_ALL, V_PAD), lambda h, i: (h, 0, 0))],
        out_specs=pl.BlockSpec((ATT_TQ, MLA_V), lambda h, i: (i, h)),
        out_shape=jax.ShapeDtypeStruct((SEQ, MLA_WIDTH), F32),
        compiler_params=_cparams("parallel", "parallel"),
        name="attention_latent",
    )(q, k, v)
    cblk = SEQ // CTX_LEN
    y_c = pl.pallas_call(
        _attention_context_kernel,
        grid=(MLA_HEADS,),
        in_specs=[pl.BlockSpec((1, CTX_LEN, QK_PAD), lambda h: (h, cblk, 0)),
                  pl.BlockSpec((1, CTX_LEN, QK_PAD), lambda h: (h, cblk, 0)),
                  pl.BlockSpec((1, CTX_LEN, V_PAD), lambda h: (h, cblk, 0))],
        out_specs=pl.BlockSpec((CTX_LEN, MLA_V), lambda h: (0, h)),
        out_shape=jax.ShapeDtypeStruct((CTX_LEN, MLA_WIDTH), F32),
        compiler_params=_cparams("parallel"),
        name="attention_context",
    )(q, k, v)
    return jnp.concatenate([y_l, y_c], axis=0)


OUT_TM = 384


def _outproj_kernel(ys_ref, ym_ref, yh_ref, x_ref, mod_ref, nw_ref, w_ref, r_ref, x1_ref, h2_ref, aff_ref):
    y = _dot(ys_ref[...].astype(BF16), w_ref[0:SSD_WIDTH, :])
    y += _dot(ym_ref[...].astype(BF16), w_ref[SSD_WIDTH:SSD_WIDTH + MLA_WIDTH, :])
    y += _dot(yh_ref[...].astype(BF16), w_ref[SSD_WIDTH + MLA_WIDTH:D_MODEL, :])
    row0 = pl.program_id(0) * OUT_TM
    is_ctx = (row0 + lax.broadcasted_iota(jnp.int32, (OUT_TM, 1), 0)) >= SEQ
    gate = jnp.where(is_ctx, mod_ref[1:2, 2 * D_MODEL:3 * D_MODEL], mod_ref[0:1, 2 * D_MODEL:3 * D_MODEL])
    x1 = x_ref[...] + gate * y
    x1_ref[...] = x1
    h2 = _norm_mod(x1, nw_ref[...], mod_ref, row0, 3, 4)
    h2_ref[...] = h2
    logits = _dot_hp(h2, r_ref[...])
    lane = lax.broadcasted_iota(jnp.int32, logits.shape, 1)
    logits = jnp.where(lane < N_EXPERTS, logits, -1e30)
    e = jnp.exp(logits - jnp.max(logits, axis=-1, keepdims=True))
    aff_ref[...] = e / jnp.sum(e, axis=-1, keepdims=True)


def out_projection(y_ssd, y_mla, y_hy, x, mod, nw, w, router):
    rows = lambda i: (i, 0)
    const = lambda i: (0, 0)
    return pl.pallas_call(
        _outproj_kernel,
        grid=(T_ALL // OUT_TM,),
        in_specs=[pl.BlockSpec((OUT_TM, SSD_WIDTH), rows),
                  pl.BlockSpec((OUT_TM, MLA_WIDTH), rows),
                  pl.BlockSpec((OUT_TM, HY_WIDTH), rows),
                  pl.BlockSpec((OUT_TM, D_MODEL), rows),
                  pl.BlockSpec((8, N_MOD * D_MODEL), const),
                  pl.BlockSpec((1, D_MODEL), const),
                  pl.BlockSpec((D_MODEL, D_MODEL), const),
                  pl.BlockSpec((D_MODEL, 128), const)],
        out_specs=[pl.BlockSpec((OUT_TM, D_MODEL), rows),
                   pl.BlockSpec((OUT_TM, D_MODEL), rows),
                   pl.BlockSpec((OUT_TM, 128), rows)],
        out_shape=[jax.ShapeDtypeStruct((T_ALL, D_MODEL), F32),
                   jax.ShapeDtypeStruct((T_ALL, D_MODEL), F32),
                   jax.ShapeDtypeStruct((T_ALL, 128), F32)],
        compiler_params=_cparams("parallel"),
        name="out_projection",
    )(y_ssd, y_mla, y_hy, x, mod, nw, w, router)


def _final_norm_kernel(x_ref, w_ref, o_ref):
    x = x_ref[...]
    o_ref[...] = x * lax.rsqrt(jnp.mean(x * x, axis=-1, keepdims=True) + EPS) * w_ref[...]


def final_norm(x, w):
    tm = 512
    return pl.pallas_call(
        _final_norm_kernel,
        grid=(SEQ // tm,),
        in_specs=[pl.BlockSpec((tm, D_MODEL), lambda i: (i, 0)), pl.BlockSpec((1, D_MODEL), lambda i: (0, 0))],
        out_specs=pl.BlockSpec((tm, D_MODEL), lambda i: (i, 0)),
        out_shape=jax.ShapeDtypeStruct((SEQ, D_MODEL), F32),
        compiler_params=_cparams("parallel"),
        name="final_norm",
    )(x, w)


CONV_TM = 256
CONV_HALO = 8


def _conv_kernel(main_ref, prev_ref, next_ref, w_ref, b_ref, o_ref, ext_ref, *, taps, act):
    i = pl.program_id(0)
    n_lat = SEQ // CONV_TM
    has_prev = jnp.logical_and(i != 0, i != n_lat)
    has_next = jnp.logical_and(i != n_lat - 1, i != n_lat)
    ext_ref[0:CONV_HALO, :] = jnp.where(has_prev, prev_ref[...], 0.0)
    ext_ref[CONV_HALO:CONV_HALO + CONV_TM, :] = main_ref[...]
    ext_ref[CONV_HALO + CONV_TM:2 * CONV_HALO + CONV_TM, :] = jnp.where(has_next, next_ref[...], 0.0)
    first = CONV_HALO - taps // 2
    acc = w_ref[0:1, :] * ext_ref[first:first + CONV_TM, :] + b_ref[...]
    for k in range(1, taps):
        acc = acc + w_ref[k:k + 1, :] * ext_ref[first + k:first + k + CONV_TM, :]
    o_ref[...] = _silu(acc) if act else acc


def dwconv(p, col_block, width, w, b, act):
    taps = w.shape[0]
    r = CONV_TM // CONV_HALO
    last = T_ALL // CONV_HALO - 1
    return pl.pallas_call(
        functools.partial(_conv_kernel, taps=taps, act=act),
        grid=(T_ALL // CONV_TM,),
        in_specs=[pl.BlockSpec((CONV_TM, width), lambda i: (i, col_block)),
                  pl.BlockSpec((CONV_HALO, width), lambda i: (jnp.maximum(i * r - 1, 0), col_block)),
                  pl.BlockSpec((CONV_HALO, width), lambda i: (jnp.minimum((i + 1) * r, last), col_block)),
                  pl.BlockSpec((taps, width), lambda i: (0, 0)),
                  pl.BlockSpec((1, width), lambda i: (0, 0))],
        out_specs=pl.BlockSpec((CONV_TM, width), lambda i: (i, 0)),
        out_shape=jax.ShapeDtypeStruct((T_ALL, width), F32),
        scratch_shapes=[pltpu.VMEM((CONV_TM + 2 * CONV_HALO, width), F32)],
        compiler_params=_cparams("parallel"),
        name="dwconv",
    )(p, p, p, w, b)


N_CHUNK = T_ALL // SSD_CHUNK
N_CTX_CHUNK = CTX_LEN // SSD_CHUNK
SSD_PAIRS = SSD_HEADS // 2


def _ssd_kernel(*refs, reverse):
    if reverse:
        (xbc_ref, dt_ref, bias_ref, alog_ref, tri_ref, yf_ref, z_ref, dskip_ref, nw_ref, y_ref, s_ref) = refs
    else:
        (xbc_ref, dt_ref, bias_ref, alog_ref, tri_ref, y_ref, s_ref) = refs
    q = SSD_CHUNK

    @pl.when(pl.program_id(0) == 0)
    def _():
        s_ref[...] = jnp.zeros_like(s_ref)

    dtraw = dt_ref[...]
    if reverse:
        dtraw = pltpu.roll(dtraw, 128 - SSD_HEADS, 1)
    v = dtraw + bias_ref[...]
    dt = jnp.maximum(v, 0.0) + jnp.log(1.0 + jnp.exp(-jnp.abs(v)))
    da = dt * (-jnp.exp(alog_ref[...]))
    d1, d2, d3 = _split3(da)
    tri = tri_ref[...]
    acum = _dot(tri, d1) + _dot(tri, d2) + _dot(tri, d3)
    acum_t = acum.T
    dt_t = dt.T
    last = 0 if reverse else q - 1
    ii = lax.broadcasted_iota(jnp.int32, (q, q), 0)
    jj = lax.broadcasted_iota(jnp.int32, (q, q), 1)
    mask = (jj >= ii) if reverse else (jj <= ii)
    left = lax.broadcasted_iota(jnp.int32, (q, 128), 1) < SSD_HEAD_DIM
    left_row = lax.broadcasted_iota(jnp.int32, (1, 128), 1) < SSD_HEAD_DIM

    for g in range(SSD_GROUPS):
        bg = xbc_ref[:, SSD_WIDTH + g * SSD_STATE:SSD_WIDTH + (g + 1) * SSD_STATE]
        cg = xbc_ref[:, SSD_WIDTH + (SSD_GROUPS + g) * SSD_STATE:SSD_WIDTH + (SSD_GROUPS + g + 1) * SSD_STATE]
        bgb = bg.astype(BF16)
        cgb = cg.astype(BF16)
        cb = lax.dot_general(cgb, bgb, (((1,), (1,)), ((), ())), preferred_element_type=F32)
        bg_t = bg.T.astype(BF16)
        for pr in range(g * SSD_PAIRS // SSD_GROUPS, (g + 1) * SSD_PAIRS // SSD_GROUPS):
            sl = slice(pr * 128, (pr + 1) * 128)
            xp = xbc_ref[:, sl]
            xpb = xp.astype(BF16)
            heads = (2 * pr, 2 * pr + 1)

            def scores(h):
                seg = acum[:, h:h + 1] - acum_t[h:h + 1, :]
                decay = jnp.exp(jnp.where(mask, seg, -1e30))
                return (cb * decay * dt_t[h:h + 1, :]).astype(BF16)

            y_diag = jnp.where(left, _dot(scores(heads[0]), xpb), _dot(scores(heads[1]), xpb))
            col = [acum[:, h:h + 1] for h in heads]
            tot = [acum_t[h:h + 1, last:last + 1] for h in heads]
            sp = s_ref[:, sl]
            y_off = _dot(cgb, sp.astype(BF16)) * jnp.where(left, jnp.exp(col[0]), jnp.exp(col[1]))
            y_ref[:, sl] = y_diag + y_off
            wgt = jnp.where(left, jnp.exp(tot[0] - col[0]) * dt[:, heads[0]:heads[0] + 1],
                            jnp.exp(tot[1] - col[1]) * dt[:, heads[1]:heads[1] + 1])
            dec = jnp.where(left_row, jnp.exp(tot[0]), jnp.exp(tot[1]))
            s_ref[:, sl] = sp * dec + _dot(bg_t, (xp * wgt).astype(BF16))

    if reverse:
        y = (y_ref[...] + yf_ref[...] + dskip_ref[...] * xbc_ref[:, 0:SSD_WIDTH]) * _silu(z_ref[...])
        y_ref[...] = y * lax.rsqrt(jnp.mean(y * y, axis=-1, keepdims=True) + EPS) * nw_ref[...]


def ssd_scan(xbc, p, bias, alog, tri, reverse, yf=None, dskip=None, nw=None):
    if reverse:
        order = lambda s: N_CHUNK - 1 - s
    else:
        order = lambda s: jnp.where(s < N_CTX_CHUNK, N_CHUNK - N_CTX_CHUNK + s, s - N_CTX_CHUNK)
    const = lambda s: (0, 0)
    in_specs = [pl.BlockSpec((SSD_CHUNK, SSD_XBC), lambda s: (order(s), 0)),
                pl.BlockSpec((SSD_CHUNK, 128), lambda s: (order(s), P_DT // 128)),
                pl.BlockSpec((1, 128), const),
                pl.BlockSpec((1, 128), const),
                pl.BlockSpec((SSD_CHUNK, SSD_CHUNK), const)]
    args = [xbc, p, bias, alog, tri]
    if reverse:
        in_specs += [pl.BlockSpec((SSD_CHUNK, SSD_WIDTH), lambda s: (order(s), 0)),
                     pl.BlockSpec((SSD_CHUNK, SSD_WIDTH), lambda s: (order(s), P_Z // SSD_WIDTH)),
                     pl.BlockSpec((1, SSD_WIDTH), const),
                     pl.BlockSpec((1, SSD_WIDTH), const)]
        args += [yf, p, dskip, nw]
    return pl.pallas_call(
        functools.partial(_ssd_kernel, reverse=reverse),
        grid=(N_CHUNK,),
        in_specs=in_specs,
        out_specs=pl.BlockSpec((SSD_CHUNK, SSD_WIDTH), lambda s: (order(s), 0)),
        out_shape=jax.ShapeDtypeStruct((T_ALL, SSD_WIDTH), F32),
        scratch_shapes=[pltpu.VMEM((SSD_STATE, SSD_WIDTH), F32)],
        compiler_params=_cparams("arbitrary"),
        name="ssd_scan_bwd" if reverse else "ssd_scan_fwd",
    )(*args)


ROUTE_TB = 256
WS_ROWS = 72


def _route_kernel(aff_ref, lstrict_ref, pos_ref, ws_ref):
    lstrict = lstrict_ref[...]
    ws_ref[...] = jnp.zeros_like(ws_ref)

    def stream(row0, nrows, k, slot_base, ws_row0):
        nblk = nrows // ROUTE_TB

        def bits(b):
            r0 = pl.multiple_of(row0 + b * ROUTE_TB, ROUTE_TB)
            return r0, pltpu.bitcast(aff_ref[pl.ds(r0, ROUTE_TB), :], jnp.int32)

        def count(pred):
            def body(b, acc):
                return acc + jnp.sum(jnp.where(pred(bits(b)[1]), 1.0, 0.0), axis=0, keepdims=True)
            return lax.fori_loop(0, nblk, body, jnp.zeros((1, 128), F32))

        def search(it, thr):
            cand = thr | jnp.left_shift(jnp.int32(1), 30 - it)
            return jnp.where(count(lambda v: v >= cand) >= k, cand, thr)

        thr = lax.fori_loop(0, 31, search, jnp.zeros((1, 128), jnp.int32))
        need = k - count(lambda v: v > thr)

        def place(b, carry):
            ties_before, picked_before = carry
            r0, v = bits(b)
            gt = v > thr
            eq = v == thr
            eqf = jnp.where(eq, 1.0, 0.0)
            tie_rank = _dot(lstrict, eqf.astype(BF16)) + ties_before
            sel = jnp.logical_or(gt, jnp.logical_and(eq, tie_rank < need))
            self_ = jnp.where(sel, 1.0, 0.0)
            rank = _dot(lstrict, self_.astype(BF16)) + picked_before
            pos_ref[pl.ds(r0, ROUTE_TB), :] = jnp.where(sel, rank + slot_base, -1.0).astype(jnp.int32)
            w = ws_row0 + 2 * b
            ws_ref[pl.ds(w, 1), :] = (picked_before + slot_base).astype(jnp.int32)
            ws_ref[pl.ds(w + 1, 1), :] = (rank[128:129, :] + slot_base).astype(jnp.int32)
            return (ties_before + jnp.sum(eqf, axis=0, keepdims=True),
                    picked_before + jnp.sum(self_, axis=0, keepdims=True))

        zero = jnp.zeros((1, 128), F32)
        lax.fori_loop(0, nblk, place, (zero, zero))

    stream(0, SEQ, CAP_L, 0.0, 0)
    stream(SEQ, CTX_LEN, CAP_C, float(CAP_L), SEQ // 128)


def route(aff, lstrict):
    return pl.pallas_call(
        _route_kernel,
        out_shape=[jax.ShapeDtypeStruct((T_ALL, 128), jnp.int32),
                   jax.ShapeDtypeStruct((WS_ROWS, 128), jnp.int32)],
        compiler_params=pltpu.CompilerParams(vmem_limit_bytes=VMEM_LIMIT),
        name="route",
    )(aff, lstrict)


ACC_ROWS = 1280


def _compact_kernel(ws_ref, pos_ref, tv_ref, acc_ref):
    acc_ref[...] = jnp.zeros_like(acc_ref)
    tv = tv_ref[...]
    lane = lax.broadcasted_iota(jnp.int32, (1, 128), 1)
    srow = lax.broadcasted_iota(jnp.int32, (256, 1), 0)

    def body(tb, carry):
        r0 = pl.multiple_of(tb * 128, 128)
        slot_of_token = pos_ref[pl.ds(r0, 128), :].astype(F32).T
        scale = jnp.where(lane == 0, 1.0, jnp.where(lane == 1, jnp.asarray(tb * 128, F32), 0.0))
        for e in range(N_EXPERTS):
            base = pl.multiple_of(jnp.minimum(ws_ref[tb, e] & -128, ACC_ROWS - 256), 128)
            onehot = jnp.where(slot_of_token[e:e + 1, :] == (srow + base).astype(F32), 1.0, 0.0)
            found = _dot(onehot.astype(BF16), tv)
            acc_ref[e, pl.ds(base, 256), :] += found * scale
        return carry

    lax.fori_loop(0, N_TB, body, 0)


def compact(ws, pos, tv):
    return pl.pallas_call(
        _compact_kernel,
        grid_spec=pltpu.PrefetchScalarGridSpec(
            num_scalar_prefetch=1,
            grid=(1,),
            in_specs=[pl.BlockSpec((T_ALL, 128), lambda i, ws: (0, 0)),
                      pl.BlockSpec((128, 128), lambda i, ws: (0, 0))],
            out_specs=pl.BlockSpec((N_EXPERTS, ACC_ROWS, 128), lambda i, ws: (0, 0, 0))),
        out_shape=jax.ShapeDtypeStruct((N_EXPERTS, ACC_ROWS, 128), F32),
        compiler_params=_cparams("arbitrary"),
        name="compact",
    )(ws, pos, tv)


FFN_TF = 256
GATHER_ROWS = 128
N_GATHER = SLOTS // GATHER_ROWS


def _ffn_kernel(idx_ref, h_ref, w1_ref, w3_ref, w2_ref, y_ref, x_ref, acc_ref, gbuf_ref, sem_ref):
    e = pl.program_id(0)
    f = pl.program_id(1)

    def row_copy(chunk, slot, r, token):
        return pltpu.make_async_copy(h_ref.at[pl.ds(token, 1), :], gbuf_ref.at[slot, pl.ds(r, 1), :],
                                     sem_ref.at[slot])

    def issue(chunk, slot):
        def body(r, carry):
            row_copy(chunk, slot, r, idx_ref[e, chunk * GATHER_ROWS + r]).start()
            return carry
        lax.fori_loop(0, GATHER_ROWS, body, 0)

    def drain(chunk, slot):
        def body(r, carry):
            row_copy(chunk, slot, r, 0).wait()
            return carry
        lax.fori_loop(0, GATHER_ROWS, body, 0)
        x_ref[chunk * GATHER_ROWS:(chunk + 1) * GATHER_ROWS, :] = gbuf_ref[slot].astype(BF16)

    @pl.when(f == 0)
    def _():
        issue(0, 0)
        for chunk in range(N_GATHER):
            if chunk + 1 < N_GATHER:
                issue(chunk + 1, (chunk + 1) % 2)
            drain(chunk, chunk % 2)

    x = x_ref[...]
    a = _dot(x, w1_ref[...].astype(BF16))
    b = _dot(x, w3_ref[...].astype(BF16))
    hid = (_silu(a) * b).astype(BF16)
    part = _dot(hid, w2_ref[...].astype(BF16))

    @pl.when(f == 0)
    def _():
        acc_ref[...] = part

    @pl.when(f > 0)
    def _():
        acc_ref[...] += part

    @pl.when(f == EXPERT_FF // FFN_TF - 1)
    def _():
        y_ref[...] = acc_ref[...].astype(BF16)


def expert_ffn(idx, h2, w1, w3, w2, layer):
    return pl.pallas_call(
        _ffn_kernel,
        grid_spec=pltpu.PrefetchScalarGridSpec(
            num_scalar_prefetch=1,
            grid=(N_EXPERTS, EXPERT_FF // FFN_TF),
            in_specs=[pl.BlockSpec(memory_space=pl.ANY),
                      pl.BlockSpec((None, None, D_MODEL, FFN_TF), lambda e, f, idx: (layer, e, 0, f)),
                      pl.BlockSpec((None, None, D_MODEL, FFN_TF), lambda e, f, idx: (layer, e, 0, f)),
                      pl.BlockSpec((None, None, FFN_TF, D_MODEL), lambda e, f, idx: (layer, e, f, 0))],
            out_specs=pl.BlockSpec((None, SLOTS, D_MODEL), lambda e, f, idx: (e, 0, 0)),
            scratch_shapes=[pltpu.VMEM((SLOTS, D_MODEL), BF16),
                            pltpu.VMEM((SLOTS, D_MODEL), F32),
                            pltpu.VMEM((2, GATHER_ROWS, D_MODEL), F32),
                            pltpu.SemaphoreType.DMA((2,))]),
        out_shape=jax.ShapeDtypeStruct((N_EXPERTS, SLOTS, D_MODEL), BF16),
        compiler_params=_cparams("arbitrary", "arbitrary"),
        name="expert_ffn",
    )(idx, h2, w1, w3, w2)


CMB_TD = 512
CMB_WIN = 256


def _combine_kernel(ws_ref, y_ref, pos_ref, aff_ref, x1_ref, g_ref, o_ref):
    tb = pl.program_id(1)
    pos = pos_ref[...]
    aff = aff_ref[...]
    lane = lax.broadcasted_iota(jnp.int32, (1, CMB_WIN), 1)
    acc = jnp.zeros((128, CMB_TD), F32)
    for e in range(N_EXPERTS):
        base = pl.multiple_of(jnp.minimum(ws_ref[tb, e] & -128, SLOTS - CMB_WIN), 128)
        onehot = jnp.where(pos[:, e:e + 1] == lane + base, 1.0, 0.0).astype(BF16)
        acc = acc + aff[:, e:e + 1] * _dot(onehot, y_ref[e, pl.ds(base, CMB_WIN), :])
    g = jnp.where(tb >= SEQ // 128, g_ref[1:2, :], g_ref[0:1, :])
    o_ref[...] = x1_ref[...] + g * acc


def combine(ws, y, pos, aff, x1, mod):
    nd = D_MODEL // CMB_TD
    return pl.pallas_call(
        _combine_kernel,
        grid_spec=pltpu.PrefetchScalarGridSpec(
            num_scalar_prefetch=1,
            grid=(nd, N_TB),
            in_specs=[pl.BlockSpec((N_EXPERTS, SLOTS, CMB_TD), lambda d, t, ws: (0, 0, d)),
                      pl.BlockSpec((128, 128), lambda d, t, ws: (t, 0)),
                      pl.BlockSpec((128, 128), lambda d, t, ws: (t, 0)),
                      pl.BlockSpec((128, CMB_TD), lambda d, t, ws: (t, d)),
                      pl.BlockSpec((8, CMB_TD), lambda d, t, ws: (0, 5 * nd + d))],
            out_specs=pl.BlockSpec((128, CMB_TD), lambda d, t, ws: (t, d))),
        out_shape=jax.ShapeDtypeStruct((T_ALL, D_MODEL), F32),
        compiler_params=_cparams("arbitrary", "arbitrary"),
        name="combine",
    )(ws, y, pos, aff, x1, mod)


HF_TM = 256
HY_COLS = HY_ORDER * HY_WIDTH


def _split2(a):
    hi = a.astype(BF16)
    return hi, (a - hi.astype(F32)).astype(BF16)


def _dot3(a, b):
    return _dot(a[0], b[0]) + (_dot(a[0], b[1]) + _dot(a[1], b[0]))


def _hy_filter_kernel(w1_ref, b1_ref, w2_ref, b2_ref, w3_ref, b3_ref, wo_ref, fr_ref, bands_ref, deltas_ref,
                      k_ref, sum_ref, *, seq):
    i = pl.program_id(0)
    n = i * HF_TM + lax.broadcasted_iota(jnp.int32, (HF_TM, 1), 0)
    lag = jnp.where(n < seq, n, jnp.where(n == seq, 0, 2 * seq - n)).astype(F32)
    t = lag / (seq - 1.0)
    ang = (2.0 * math.pi * lag) * bands_ref[...] / float(seq)
    lane = lax.broadcasted_iota(jnp.int32, (1, 128), 1)
    feats = jnp.where(lane == 0, t, jnp.where(lane <= HY_BANDS, jnp.cos(ang),
                                               jnp.where(lane <= 2 * HY_BANDS, -jnp.sin(ang), 0.0)))
    fr = fr_ref[...]
    hdn = jnp.sin(fr * (_dot_hp(feats, w1_ref[...]) + b1_ref[...]))
    hdn = jnp.sin(fr * (_dot_hp(hdn, w2_ref[...]) + b2_ref[...]))
    hdn = jnp.sin(fr * (_dot_hp(hdn, w3_ref[...]) + b3_ref[...]))
    h = _dot_hp(hdn, wo_ref[...]) * jnp.exp(-t * deltas_ref[...])

    @pl.when(i == 0)
    def _():
        sum_ref[...] = jnp.zeros_like(sum_ref)

    sum_ref[0:1, :] += jnp.sum(jnp.abs(h), axis=0, keepdims=True)
    k_ref[...] = jnp.where(n == seq, 0.0, h)


def hy_filter(seq, w1, b1, w2, b2, w3, b3, wo, fr, bands, deltas):
    half = seq // HF_TM
    const = lambda i: (0, 0)
    sq = pl.BlockSpec((128, 128), const)
    row = pl.BlockSpec((1, 128), const)
    return pl.pallas_call(
        functools.partial(_hy_filter_kernel, seq=seq),
        grid=(2 * half,),
        in_specs=[sq, row, sq, row, sq, row,
                  pl.BlockSpec((128, HY_COLS), lambda i: (0, jnp.where(i >= half, 1, 0))),
                  row, row, pl.BlockSpec((1, HY_COLS), const)],
        out_specs=[pl.BlockSpec((HF_TM, HY_COLS), lambda i: (i, 0)),
                   pl.BlockSpec((8, HY_COLS), const)],
        out_shape=[jax.ShapeDtypeStruct((2 * seq, HY_COLS), F32),
                   jax.ShapeDtypeStruct((8, HY_COLS), F32)],
        compiler_params=_cparams("arbitrary"),
        name="hy_filter",
    )(w1, b1, w2, b2, w3, b3, wo, fr, bands, deltas)


FFT_R = 128
FFT_N = FFT_R * FFT_R
DFT_TN = 4096


def _dft1_kernel(x_ref, c_ref, s_ref, re_ref, im_ref):
    xp = _split2(x_ref[...])
    re_ref[...] = _dot3(_split2(c_ref[...]), xp)
    im_ref[...] = -_dot3(_split2(s_ref[...]), xp)


def dft_rows(x2d, cmat, smat):
    k, m = x2d.shape
    const = lambda j: (0, 0)
    return pl.pallas_call(
        _dft1_kernel,
        grid=(m // DFT_TN,),
        in_specs=[pl.BlockSpec((k, DFT_TN), lambda j: (0, j)),
                  pl.BlockSpec((FFT_R, k), const), pl.BlockSpec((FFT_R, k), const)],
        out_specs=[pl.BlockSpec((FFT_R, DFT_TN), lambda j: (0, j))] * 2,
        out_shape=[jax.ShapeDtypeStruct((FFT_R, m), F32)] * 2,
        compiler_params=_cparams("parallel"),
        name="hy_dft_rows",
    )(x2d, cmat, smat)


def _twiddle(re, im, tc, ts, conj):
    if conj:
        return re * tc - im * ts, im * tc + re * ts
    return re * tc + im * ts, im * tc - re * ts


def _cdft(cp, sp, re, im, inverse):
    rp = _split2(re)
    ip = _split2(im)
    if inverse:
        return _dot3(cp, rp) - _dot3(sp, ip), _dot3(cp, ip) + _dot3(sp, rp)
    return _dot3(cp, rp) + _dot3(sp, ip), _dot3(cp, ip) - _dot3(sp, rp)


def _mid_filter_kernel(are_ref, aim_ref, twc_ref, tws_ref, c_ref, s_ref, kre_ref, kim_ref):
    re, im = _twiddle(are_ref[0], aim_ref[0], twc_ref[0], tws_ref[0], False)
    kre_ref[0], kim_ref[0] = _cdft(_split2(c_ref[...]), _split2(s_ref[...]), re, im, False)


def _mid_data_kernel(are_ref, aim_ref, kre_ref, kim_ref, twc_ref, tws_ref, c_ref, s_ref, bre_ref, bim_ref):
    cp = _split2(c_ref[...])
    sp = _split2(s_ref[...])
    tc = twc_ref[0]
    ts = tws_ref[0]
    re, im = _twiddle(are_ref[0], aim_ref[0], tc, ts, False)
    xr, xi = _cdft(cp, sp, re, im, False)
    kr = kre_ref[0]
    ki = kim_ref[0]
    br, bi = _cdft(cp, sp, xr * kr - xi * ki, xr * ki + xi * kr, True)
    bre_ref[0], bim_ref[0] = _twiddle(br, bi, tc, ts, True)


def fft_mid(are, aim, twc, tws, cmat, smat, kf=None, order=0):
    width = are.shape[-1]
    slab = pl.BlockSpec((1, FFT_R, width), lambda k1: (k1, 0, 0))
    tw = pl.BlockSpec((1, FFT_R, 1), lambda k1: (k1, 0, 0))
    sq = pl.BlockSpec((FFT_R, FFT_R), lambda k1: (0, 0))
    if kf is None:
        body, ins, specs = _mid_filter_kernel, (are, aim, twc, tws, cmat, smat), [slab, slab, tw, tw, sq, sq]
    else:
        kslab = pl.BlockSpec((1, FFT_R, width), lambda k1: (k1, 0, order))
        body, ins = _mid_data_kernel, (are, aim, kf[0], kf[1], twc, tws, cmat, smat)
        specs = [slab, slab, kslab, kslab, tw, tw, sq, sq]
    return pl.pallas_call(
        body,
        grid=(FFT_R,),
        in_specs=specs,
        out_specs=[slab, slab],
        out_shape=[jax.ShapeDtypeStruct((FFT_R, FFT_R, width), F32)] * 2,
        compiler_params=_cparams("parallel"),
        name="hy_fft_mid_filter" if kf is None else "hy_fft_mid",
    )(*ins)


def _idft1_kernel(bre_ref, bim_ref, c_ref, s_ref, gate_ref, zin_ref, bias_ref, sum_ref, o_ref):
    conv = (_dot3(_split2(c_ref[...]), _split2(bre_ref[...]))
            - _dot3(_split2(s_ref[...]), _split2(bim_ref[...])))
    o_ref[...] = gate_ref[...] * (conv / (FFT_N * sum_ref[...]) + bias_ref[...] * zin_ref[...])


def idft_rows_gated(bre, bim, cmat, smat, gate, zin, bias_t, sum_t):
    rows, m = gate.shape
    const = lambda j: (0, 0)
    tile = lambda r: pl.BlockSpec((r, DFT_TN), lambda j: (0, j))
    return pl.pallas_call(
        _idft1_kernel,
        grid=(m // DFT_TN,),
        in_specs=[tile(FFT_R), tile(FFT_R), pl.BlockSpec((rows, FFT_R), const), pl.BlockSpec((rows, FFT_R), const),
                  tile(rows), tile(rows), pl.BlockSpec((1, DFT_TN), const), pl.BlockSpec((1, DFT_TN), const)],
        out_specs=tile(rows),
        out_shape=jax.ShapeDtypeStruct((rows, m), F32),
        compiler_params=_cparams("parallel"),
        name="hy_idft_rows",
    )(bre, bim, cmat, smat, gate, zin, bias_t, sum_t)


CTX_N = 2 * CTX_LEN


def _hy_ctx_kernel(u_ref, k_ref, c_ref, s_ref, gate_ref, bias_ref, sum_ref, o_ref):
    cp = _split2(c_ref[...])
    sp = _split2(s_ref[...])
    head = lambda pair: (pair[0][:, 0:CTX_LEN], pair[1][:, 0:CTX_LEN])
    top = lambda pair: (pair[0][0:CTX_LEN, :], pair[1][0:CTX_LEN, :])
    u = u_ref[...]
    up = _split2(u)
    ur = _dot3(head(cp), up)
    ui = -_dot3(head(sp), up)
    kp = _split2(k_ref[...])
    kr = _dot3(cp, kp)
    ki = -_dot3(sp, kp)
    conv = (_dot3(top(cp), _split2(ur * kr - ui * ki)) - _dot3(top(sp), _split2(ur * ki + ui * kr)))
    o_ref[...] = gate_ref[...] * (conv / (CTX_N * sum_ref[...]) + bias_ref[...] * u)


def hy_ctx_conv(u, kfilt, order, cmat, smat, gate, bias, sums):
    const = lambda i: (0, 0)
    chan = pl.BlockSpec((CTX_LEN, HY_WIDTH), const)
    vec = pl.BlockSpec((1, HY_WIDTH), const)
    return pl.pallas_call(
        _hy_ctx_kernel,
        grid=(1,),
        in_specs=[chan, pl.BlockSpec((CTX_N, HY_WIDTH), lambda i: (0, order)),
                  pl.BlockSpec((CTX_N, CTX_N), const), pl.BlockSpec((CTX_N, CTX_N), const), chan, vec, vec],
        out_specs=chan,
        out_shape=jax.ShapeDtypeStruct((CTX_LEN, HY_WIDTH), F32),
        compiler_params=_cparams("arbitrary"),
        name="hy_ctx_conv",
    )(u, kfilt, cmat, smat, gate, bias, sums)


def _rope_tables():
    rows = SEQ // GRID_W
    r, col = jnp.meshgrid(jnp.arange(rows), jnp.arange(GRID_W), indexing='ij')
    per_axis = MLA_ROPE // 2
    inv = ROPE_THETA ** (-jnp.arange(0, per_axis, 2, dtype=F32) / per_axis)
    ang = jnp.concatenate([r.reshape(-1, 1) * inv, col.reshape(-1, 1) * inv], axis=-1)
    cos = jnp.concatenate([jnp.cos(ang), jnp.ones((CTX_LEN, per_axis), F32)], axis=0)
    sin = jnp.concatenate([jnp.sin(ang), jnp.zeros((CTX_LEN, per_axis), F32)], axis=0)
    z = jnp.zeros_like(cos)
    return jnp.concatenate([cos, z, cos, z], axis=-1), jnp.concatenate([-sin, z, sin, z], axis=-1)


def _permute_w_in(w_in):
    z, xbc, dt, cq, ckv, kr, hy = jnp.split(w_in, np.cumsum(
        (SSD_WIDTH, SSD_XBC, 2 * SSD_HEADS, MLA_Q_RANK, MLA_KV_RANK, MLA_ROPE)).tolist(), axis=-1)
    zeros = lambda n: jnp.zeros(w_in.shape[:-1] + (n,), w_in.dtype)
    return jnp.concatenate([hy, z, ckv, xbc, kr[..., :32], zeros(32), kr[..., 32:], zeros(32),
                            dt, zeros(128 - 2 * SSD_HEADS), cq], axis=-1).astype(BF16)


def _pad_w_uq(w_uq):
    w = w_uq.reshape(DEPTH, MLA_Q_RANK, MLA_HEADS, MLA_QK)
    zeros = jnp.zeros(w.shape[:-1] + (32,), w.dtype)
    w = jnp.concatenate([w[..., :MLA_NOPE], w[..., MLA_NOPE:MLA_NOPE + 32], zeros,
                         w[..., MLA_NOPE + 32:], zeros], axis=-1)
    return w.reshape(DEPTH, MLA_Q_RANK, MLA_HEADS * QK_PAD).astype(BF16)


def kernel(x, c, ctx, c_ctx, w_mod, b_mod, norm1_w, norm2_w, w_in, w_out, ssd_conv_w, ssd_conv_b,
           ssd_dt_bias, ssd_a_log, ssd_d, ssd_norm_w, mla_q_norm_w, mla_w_uq, mla_kv_norm_w, mla_w_ukv,
           hy_short_w, hy_short_b, hy_w1, hy_b1, hy_w2, hy_b2, hy_w3, hy_b3, hy_w_out, hy_freq, hy_bias,
           moe_router, moe_w1, moe_w3, moe_w2, final_norm_w):
    xj = jnp.concatenate([x[0], ctx[0]], axis=0)
    cvec = jnp.concatenate([c, c_ctx[None, :], jnp.zeros((6, D_MODEL), F32)], axis=0)
    mods = modulation(cvec, w_mod, b_mod)
    c2, s2 = _rope_tables()
    w_in_p = _permute_w_in(w_in)
    w_uq_p = _pad_w_uq(mla_w_uq)
    w_ukv_b = mla_w_ukv.astype(BF16)
    w_out_b = w_out.astype(BF16)
    router_p = jnp.pad(moe_router, ((0, 0), (0, 0), (0, 128 - N_EXPERTS)))
    lane_pad = lambda a: jnp.pad(a, ((0, 0), (0, 0), (0, 128 - SSD_HEADS)))[:, :, None, :]
    dt_bias = lane_pad(ssd_dt_bias)
    a_log = lane_pad(ssd_a_log)
    d_skip = jnp.repeat(ssd_d, SSD_HEAD_DIM, axis=-1)[:, None, :]
    tri_lower = jnp.asarray(np.tril(np.ones((SSD_CHUNK, SSD_CHUNK), np.float32)), BF16)
    tri_upper = jnp.asarray(np.triu(np.ones((SSD_CHUNK, SSD_CHUNK), np.float32)), BF16)
    lstrict = jnp.asarray(np.tril(np.ones((ROUTE_TB, ROUTE_TB), np.float32), -1), BF16)
    tv = np.zeros((128, 128), np.float32)
    tv[:, 0] = np.arange(128)
    tv[:, 1] = 1.0
    tvals = jnp.asarray(tv, BF16)

    pad_to = lambda a, r, c_: jnp.pad(a, ((0, 0), (0, r - a.shape[1]), (0, c_ - a.shape[2])))
    hf_w1, hf_w2, hf_w3 = pad_to(hy_w1, 128, 128), pad_to(hy_w2, 128, 128), pad_to(hy_w3, 128, 128)
    hf_b1, hf_b2, hf_b3, hf_fr = (pad_to(a[:, None, :], 1, 128) for a in (hy_b1, hy_b2, hy_b3, hy_freq))
    hf_wo = hy_w_out.reshape(DEPTH, HY_FFN, HY_ORDER, 2, HY_WIDTH).transpose(0, 1, 3, 2, 4)
    hf_wo = pad_to(hf_wo.reshape(DEPTH, HY_FFN, 2 * HY_COLS), 128, 2 * HY_COLS)
    band_vals = jnp.linspace(1e-4, HY_BANDS - 1, HY_BANDS, dtype=F32)
    bands = jnp.concatenate([jnp.zeros((1,), F32), band_vals, band_vals,
                             jnp.zeros((128 - HY_EMB,), F32)])[None, :]
    deltas = jnp.abs(jnp.linspace(math.log(HY_TARGET) / HY_SLOW_DECAY_PCT,
                                  math.log(HY_TARGET) / HY_FAST_DECAY_PCT, HY_WIDTH, dtype=F32))
    deltas = jnp.tile(deltas, HY_ORDER)[None, :]
    jk = np.outer(np.arange(FFT_R), np.arange(FFT_R))
    cos_r = jnp.asarray(np.cos(2 * np.pi * (jk % FFT_R) / FFT_R), F32)
    sin_r = jnp.asarray(np.sin(2 * np.pi * (jk % FFT_R) / FFT_R), F32)
    tw_cos = jnp.asarray(np.cos(2 * np.pi * jk / FFT_N), F32)[:, :, None]
    tw_sin = jnp.asarray(np.sin(2 * np.pi * jk / FFT_N), F32)[:, :, None]
    jk_c = np.outer(np.arange(CTX_N), np.arange(CTX_N)) % CTX_N
    cos_c = jnp.asarray(np.cos(2 * np.pi * jk_c / CTX_N), F32)
    sin_c = jnp.asarray(np.sin(2 * np.pi * jk_c / CTX_N), F32)

    for i in range(DEPTH):
        mod = mods[i]
        p = in_projection(xj, mod, norm1_w[i][None, :], w_in_p[i])
        q, k, v = mla_projection(p, c2, s2, mla_q_norm_w[i][None, :], mla_kv_norm_w[i][None, :],
                                 w_uq_p[i], w_ukv_b[i])
        y_mla = attention(q, k, v)

        xbc = dwconv(p, P_XBC // SSD_XBC, SSD_XBC, ssd_conv_w[i], ssd_conv_b[i][None, :], act=True)
        y_f = ssd_scan(xbc, p, dt_bias[i, 0], a_log[i, 0], tri_lower, reverse=False)
        y_ssd = ssd_scan(xbc, p, dt_bias[i, 1], a_log[i, 1], tri_upper, reverse=True, yf=y_f,
                         dskip=d_skip[i], nw=ssd_norm_w[i][None, :])

        vxx = [dwconv(p, j, HY_WIDTH, hy_short_w[i][:, j * HY_WIDTH:(j + 1) * HY_WIDTH],
                      hy_short_b[i][None, j * HY_WIDTH:(j + 1) * HY_WIDTH], act=False) for j in range(3)]
        filt = (hf_w1[i], hf_b1[i], hf_w2[i], hf_b2[i], hf_w3[i], hf_b3[i], hf_wo[i], hf_fr[i], bands, deltas)
        k_l, sum_l = hy_filter(SEQ, *filt)
        k_c, sum_c = hy_filter(CTX_LEN, *filt)
        are, aim = dft_rows(k_l.reshape(FFT_R, FFT_R * HY_COLS), cos_r, sin_r)
        kf = fft_mid(are.reshape(FFT_R, FFT_R, HY_COLS), aim.reshape(FFT_R, FFT_R, HY_COLS),
                     tw_cos, tw_sin, cos_r, sin_r)
        half = FFT_R // 2
        z_l = vxx[0][:SEQ].reshape(half, FFT_R * HY_WIDTH)
        z_c = vxx[0][SEQ:]
        for o in range(HY_ORDER):
            ch = slice(o * HY_WIDTH, (o + 1) * HY_WIDTH)
            are, aim = dft_rows(z_l, cos_r[:, :half], sin_r[:, :half])
            bre, bim = fft_mid(are.reshape(FFT_R, FFT_R, HY_WIDTH), aim.reshape(FFT_R, FFT_R, HY_WIDTH),
                               tw_cos, tw_sin, cos_r, sin_r, kf=kf, order=o)
            z_l = idft_rows_gated(bre.reshape(FFT_R, FFT_R * HY_WIDTH), bim.reshape(FFT_R, FFT_R * HY_WIDTH),
                                  cos_r[:half], sin_r[:half],
                                  vxx[1 + o][:SEQ].reshape(half, FFT_R * HY_WIDTH), z_l,
                                  jnp.tile(hy_bias[i, o][None, :], (1, DFT_TN // HY_WIDTH)),
                                  jnp.tile(sum_l[0:1, ch], (1, DFT_TN // HY_WIDTH)))
            z_c = hy_ctx_conv(z_c, k_c, o, cos_c, sin_c, vxx[1 + o][SEQ:], hy_bias[i, o][None, :], sum_c[0:1, ch])
        y_hy = jnp.concatenate([z_l.reshape(SEQ, HY_WIDTH), z_c], axis=0)

        x1, h2, aff = out_projection(y_ssd, y_mla, y_hy, xj, mod, norm2_w[i][None, :], w_out_b[i], router_p[i])
        pos, ws = route(aff, lstrict)
        found = compact(ws, pos, tvals)
        idx = (found[:, :SLOTS, 0] + found[:, :SLOTS, 1]).astype(jnp.int32)
        y_moe = expert_ffn(idx, h2, moe_w1, moe_w3, moe_w2, i)
        xj = combine(ws, y_moe, pos, aff, x1, mod)
    return final_norm(xj, final_norm_w[None, :])[None]
```

```python
import functools
import math

import jax
import jax.numpy as jnp
import numpy as np
from jax import lax
from jax.experimental import pallas as pl
from jax.experimental.pallas import tpu as pltpu

D_MODEL = 2048
SEQ = 8192
DEPTH = 4
GRID_W = 64
CTX_LEN = 256
T_ALL = SEQ + CTX_LEN
EPS = 1e-6
N_MOD = 6

SSD_HEAD_DIM = 64
SSD_WIDTH = 768
SSD_HEADS = 12
SSD_GROUPS = 2
SSD_STATE = 128
SSD_CONV = 5
SSD_CHUNK = 128
SSD_XBC = SSD_WIDTH + 2 * SSD_GROUPS * SSD_STATE
MLA_NOPE = 128
MLA_ROPE = 64
MLA_V = 128
MLA_QK = MLA_NOPE + MLA_ROPE
MLA_WIDTH = 768
MLA_HEADS = 6
MLA_Q_RANK = 512
MLA_KV_RANK = 256
ROPE_THETA = 10000.0
HY_WIDTH = 512
HY_ORDER = 2
HY_SHORT = 3
HY_BANDS = 16
HY_EMB = 1 + 2 * HY_BANDS
HY_FFN = 64
HY_TARGET = 1e-2
HY_FAST_DECAY_PCT = 0.3
HY_SLOW_DECAY_PCT = 1.5
N_EXPERTS = 16
EXPERT_FF = 1024
CAPACITY_FACTOR = 2

P_HY = 0
P_Z = 1536
P_CKV = 2304
P_XBC = 2560
P_KR = 3840
P_DT = 3968
P_CQ = 4096
P_COLS = 4608
QK_PAD = 256
V_PAD = 256
ATT_SCALE = 1.0 / math.sqrt(MLA_QK)
CAP_L = CAPACITY_FACTOR * SEQ // N_EXPERTS
CAP_C = CAPACITY_FACTOR * CTX_LEN // N_EXPERTS
SLOTS = 1152
N_TB = T_ALL // 128

F32 = jnp.float32
BF16 = jnp.bfloat16
VMEM_LIMIT = 56 * 1024 * 1024


def _cparams(*sem):
    return pltpu.CompilerParams(dimension_semantics=sem, vmem_limit_bytes=VMEM_LIMIT)


def _split3(a):
    hi = a.astype(BF16)
    r = a - hi.astype(F32)
    mid = r.astype(BF16)
    lo = (r - mid.astype(F32)).astype(BF16)
    return hi, mid, lo


def _dot(a, b):
    return jnp.dot(a, b, preferred_element_type=F32)


def _dot_hp(a, b):
    a1, a2, a3 = _split3(a)
    b1, b2, b3 = _split3(b)
    return (_dot(a1, b1) + (_dot(a1, b2) + _dot(a2, b1))
            + (_dot(a2, b2) + _dot(a1, b3) + _dot(a3, b1)))


def _silu(v):
    return v * (1.0 / (1.0 + jnp.exp(-v)))


def _mod_kernel(c_ref, w_ref, b_ref, o_ref):
    s = _silu(c_ref[...])
    o_ref[0] = _dot_hp(s, w_ref[0]) + b_ref[0]


def modulation(cvec, w_mod, b_mod):
    tn = 1024
    return pl.pallas_call(
        _mod_kernel,
        grid=(DEPTH, N_MOD * D_MODEL // tn),
        in_specs=[pl.BlockSpec((8, D_MODEL), lambda l, j: (0, 0)),
                  pl.BlockSpec((1, D_MODEL, tn), lambda l, j: (l, 0, j)),
                  pl.BlockSpec((1, 1, tn), lambda l, j: (l, 0, j))],
        out_specs=pl.BlockSpec((1, 8, tn), lambda l, j: (l, 0, j)),
        out_shape=jax.ShapeDtypeStruct((DEPTH, 8, N_MOD * D_MODEL), F32),
        compiler_params=_cparams("parallel", "parallel"),
        name="modulation",
    )(cvec, w_mod, b_mod.reshape(DEPTH, 1, N_MOD * D_MODEL))


def _norm_mod(x, nw, mod_ref, row0, k_shift, k_scale):
    n = x.shape[0]
    xn = x * lax.rsqrt(jnp.mean(x * x, axis=-1, keepdims=True) + EPS) * nw
    is_ctx = (row0 + lax.broadcasted_iota(jnp.int32, (n, 1), 0)) >= SEQ
    sl_shift = slice(k_shift * D_MODEL, (k_shift + 1) * D_MODEL)
    sl_scale = slice(k_scale * D_MODEL, (k_scale + 1) * D_MODEL)
    shift = jnp.where(is_ctx, mod_ref[1:2, sl_shift], mod_ref[0:1, sl_shift])
    scale = jnp.where(is_ctx, mod_ref[1:2, sl_scale], mod_ref[0:1, sl_scale])
    return xn * (1.0 + scale) + shift


IN_TM = 768
IN_TN = 768


def _inproj_kernel(x_ref, mod_ref, nw_ref, w_ref, o_ref, h_ref):
    @pl.when(pl.program_id(1) == 0)
    def _():
        h = _norm_mod(x_ref[...], nw_ref[...], mod_ref, pl.program_id(0) * IN_TM, 0, 1)
        h_ref[...] = h.astype(BF16)

    o_ref[...] = _dot(h_ref[...], w_ref[...])


def in_projection(x, mod, nw, w):
    return pl.pallas_call(
        _inproj_kernel,
        grid=(T_ALL // IN_TM, P_COLS // IN_TN),
        in_specs=[pl.BlockSpec((IN_TM, D_MODEL), lambda i, j: (i, 0)),
                  pl.BlockSpec((8, N_MOD * D_MODEL), lambda i, j: (0, 0)),
                  pl.BlockSpec((1, D_MODEL), lambda i, j: (0, 0)),
                  pl.BlockSpec((D_MODEL, IN_TN), lambda i, j: (0, j))],
        out_specs=pl.BlockSpec((IN_TM, IN_TN), lambda i, j: (i, j)),
        out_shape=jax.ShapeDtypeStruct((T_ALL, P_COLS), F32),
        scratch_shapes=[pltpu.VMEM((IN_TM, D_MODEL), BF16)],
        compiler_params=_cparams("parallel", "arbitrary"),
        name="in_projection",
    )(x, mod, nw, w)


MLA_TM = 768


def _rope(v, c2, s2):
    return v * c2 + pltpu.roll(v, 64, 1) * s2


def _mla_proj_kernel(cq_ref, ckv_ref, kr_ref, c2_ref, s2_ref, qnw_ref, kvnw_ref, wq_ref, wkv_ref,
                     q_ref, k_ref, v_ref):
    c2 = c2_ref[...]
    s2 = s2_ref[...]
    cq = cq_ref[...]
    cqn = cq * lax.rsqrt(jnp.mean(cq * cq, axis=-1, keepdims=True) + EPS) * qnw_ref[...]
    q = _dot(cqn.astype(BF16), wq_ref[...]) * ATT_SCALE
    ckv = ckv_ref[...]
    ckvn = ckv * lax.rsqrt(jnp.mean(ckv * ckv, axis=-1, keepdims=True) + EPS) * kvnw_ref[...]
    kv = _dot(ckvn.astype(BF16), wkv_ref[...])
    kr = _rope(kr_ref[...], c2, s2).astype(BF16)
    for h in range(MLA_HEADS):
        b = h * QK_PAD
        q_ref[h, :, 0:MLA_NOPE] = q[:, b:b + MLA_NOPE].astype(BF16)
        q_ref[h, :, MLA_NOPE:QK_PAD] = _rope(q[:, b + MLA_NOPE:b + QK_PAD], c2, s2).astype(BF16)
        k_ref[h, :, 0:MLA_NOPE] = kv[:, b:b + MLA_NOPE].astype(BF16)
        k_ref[h, :, MLA_NOPE:QK_PAD] = kr
        v_ref[h, :, 0:MLA_V] = kv[:, b + MLA_NOPE:b + QK_PAD].astype(BF16)
        v_ref[h, :, MLA_V:V_PAD] = jnp.ones((MLA_TM, V_PAD - MLA_V), BF16)


def mla_projection(p, c2, s2, qnw, kvnw, wq, wkv):
    rows = lambda i: (i, 0)
    const = lambda i: (0, 0)
    return pl.pallas_call(
        _mla_proj_kernel,
        grid=(T_ALL // MLA_TM,),
        in_specs=[pl.BlockSpec((MLA_TM, MLA_Q_RANK), lambda i: (i, P_CQ // MLA_Q_RANK)),
                  pl.BlockSpec((MLA_TM, MLA_KV_RANK), lambda i: (i, P_CKV // MLA_KV_RANK)),
                  pl.BlockSpec((MLA_TM, 128), lambda i: (i, P_KR // 128)),
                  pl.BlockSpec((MLA_TM, 128), rows),
                  pl.BlockSpec((MLA_TM, 128), rows),
                  pl.BlockSpec((1, MLA_Q_RANK), const),
                  pl.BlockSpec((1, MLA_KV_RANK), const),
                  pl.BlockSpec((MLA_Q_RANK, MLA_HEADS * QK_PAD), const),
                  pl.BlockSpec((MLA_KV_RANK, MLA_HEADS * QK_PAD), const)],
        out_specs=[pl.BlockSpec((MLA_HEADS, MLA_TM, QK_PAD), lambda i: (0, i, 0)),
                   pl.BlockSpec((MLA_HEADS, MLA_TM, QK_PAD), lambda i: (0, i, 0)),
                   pl.BlockSpec((MLA_HEADS, MLA_TM, V_PAD), lambda i: (0, i, 0))],
        out_shape=[jax.ShapeDtypeStruct((MLA_HEADS, T_ALL, QK_PAD), BF16),
                   jax.ShapeDtypeStruct((MLA_HEADS, T_ALL, QK_PAD), BF16),
                   jax.ShapeDtypeStruct((MLA_HEADS, T_ALL, V_PAD), BF16)],
        compiler_params=_cparams("parallel"),
        name="mla_projection",
    )(p, p, p, c2, s2, qnw, kvnw, wq, wkv)


ATT_TQ = 512
ATT_TK = 512


def _attn_scores(q, k):
    return lax.dot_general(q, k, (((1,), (1,)), ((), ())), preferred_element_type=F32)


def _attn_update(s, v, m, acc):
    m_new = jnp.maximum(m, jnp.max(s, axis=-1, keepdims=True))
    alpha = jnp.exp(m - m_new)
    p = jnp.exp(s - m_new).astype(BF16)
    return m_new, alpha * acc + _dot(p, v)


def _attn_finish(acc, o_ref):
    o_ref[...] = acc[:, 0:MLA_V] / acc[:, MLA_V:MLA_V + 1]


def _attention_latent_kernel(q_ref, k_ref, v_ref, o_ref, s_ref):
    q = q_ref[0]
    n = SEQ // ATT_TK

    def rows(c):
        return pl.ds(pl.multiple_of(c * ATT_TK, ATT_TK), ATT_TK)

    def body(c2, carry):
        m, acc = carry
        c = 2 * c2
        s_ref[1] = _attn_scores(q, k_ref[0, rows(c + 1), :])
        m, acc = _attn_update(s_ref[0], v_ref[0, rows(c), :], m, acc)
        s_ref[0] = _attn_scores(q, k_ref[0, rows(c + 2), :])
        return _attn_update(s_ref[1], v_ref[0, rows(c + 1), :], m, acc)

    s_ref[0] = _attn_scores(q, k_ref[0, 0:ATT_TK, :])
    init = (jnp.full((ATT_TQ, 1), -1e30, F32), jnp.zeros((ATT_TQ, V_PAD), F32))
    m, acc = lax.fori_loop(0, n // 2 - 1, body, init)
    s_ref[1] = _attn_scores(q, k_ref[0, (n - 1) * ATT_TK:n * ATT_TK, :])
    s_ctx = _attn_scores(q, k_ref[0, SEQ:T_ALL, :])
    m, acc = _attn_update(s_ref[0], v_ref[0, (n - 2) * ATT_TK:(n - 1) * ATT_TK, :], m, acc)
    m, acc = _attn_update(s_ref[1], v_ref[0, (n - 1) * ATT_TK:n * ATT_TK, :], m, acc)
    m, acc = _attn_update(s_ctx, v_ref[0, SEQ:T_ALL, :], m, acc)
    _attn_finish(acc, o_ref)


def _attention_context_kernel(q_ref, k_ref, v_ref, o_ref):
    init = (jnp.full((CTX_LEN, 1), -1e30, F32), jnp.zeros((CTX_LEN, V_PAD), F32))
    m, acc = _attn_update(_attn_scores(q_ref[0], k_ref[0]), v_ref[0], *init)
    _attn_finish(acc, o_ref)


def attention(q, k, v):
    y_l = pl.pallas_call(
        _attention_latent_kernel,
        grid=(MLA_HEADS, SEQ // ATT_TQ),
        in_specs=[pl.BlockSpec((1, ATT_TQ, QK_PAD), lambda h, i: (h, i, 0)),
                  pl.BlockSpec((1, T_ALL, QK_PAD), lambda h, i: (h, 0, 0)),
                  pl.BlockSpec((1, T_ALL, V_PAD), lambda h, i: (h, 0, 0))],
        out_specs=pl.BlockSpec((ATT_TQ, MLA_V), lambda h, i: (i, h)),
        out_shape=jax.ShapeDtypeStruct((SEQ, MLA_WIDTH), F32),
        scratch_shapes=[pltpu.VMEM((2, ATT_TQ, ATT_TK), F32)],
        compiler_params=_cparams("parallel", "parallel"),
        name="attention_latent",
    )(q, k, v)
    cblk = SEQ // CTX_LEN
    y_c = pl.pallas_call(
        _attention_context_kernel,
        grid=(MLA_HEADS,),
        in_specs=[pl.BlockSpec((1, CTX_LEN, QK_PAD), lambda h: (h, cblk, 0)),
                  pl.BlockSpec((1, CTX_LEN, QK_PAD), lambda h: (h, cblk, 0)),
                  pl.BlockSpec((1, CTX_LEN, V_PAD), lambda h: (h, cblk, 0))],
        out_specs=pl.BlockSpec((CTX_LEN, MLA_V), lambda h: (0, h)),
        out_shape=jax.ShapeDtypeStruct((CTX_LEN, MLA_WIDTH), F32),
        compiler_params=_cparams("parallel"),
        name="attention_context",
    )(q, k, v)
    return jnp.concatenate([y_l, y_c], axis=0)


OUT_TM = 384


def _outproj_kernel(ys_ref, ym_ref, yh_ref, x_ref, mod_ref, nw_ref, w_ref, r_ref, x1_ref, h2_ref, aff_ref):
    y = _dot(ys_ref[...].astype(BF16), w_ref[0:SSD_WIDTH, :])
    y += _dot(ym_ref[...].astype(BF16), w_ref[SSD_WIDTH:SSD_WIDTH + MLA_WIDTH, :])
    y += _dot(yh_ref[...].astype(BF16), w_ref[SSD_WIDTH + MLA_WIDTH:D_MODEL, :])
    row0 = pl.program_id(0) * OUT_TM
    is_ctx = (row0 + lax.broadcasted_iota(jnp.int32, (OUT_TM, 1), 0)) >= SEQ
    gate = jnp.where(is_ctx, mod_ref[1:2, 2 * D_MODEL:3 * D_MODEL], mod_ref[0:1, 2 * D_MODEL:3 * D_MODEL])
    x1 = x_ref[...] + gate * y
    x1_ref[...] = x1
    h2 = _norm_mod(x1, nw_ref[...], mod_ref, row0, 3, 4)
    h2_ref[...] = h2
    logits = _dot3(_split2(h2), _split2(r_ref[...]))
    lane = lax.broadcasted_iota(jnp.int32, logits.shape, 1)
    logits = jnp.where(lane < N_EXPERTS, logits, -1e30)
    e = jnp.exp(logits - jnp.max(logits, axis=-1, keepdims=True))
    aff_ref[...] = e / jnp.sum(e, axis=-1, keepdims=True)


def out_projection(y_ssd, y_mla, y_hy, x, mod, nw, w, router):
    rows = lambda i: (i, 0)
    const = lambda i: (0, 0)
    return pl.pallas_call(
        _outproj_kernel,
        grid=(T_ALL // OUT_TM,),
        in_specs=[pl.BlockSpec((OUT_TM, SSD_WIDTH), rows),
                  pl.BlockSpec((OUT_TM, MLA_WIDTH), rows),
                  pl.BlockSpec((OUT_TM, HY_WIDTH), rows),
                  pl.BlockSpec((OUT_TM, D_MODEL), rows),
                  pl.BlockSpec((8, N_MOD * D_MODEL), const),
                  pl.BlockSpec((1, D_MODEL), const),
                  pl.BlockSpec((D_MODEL, D_MODEL), const),
                  pl.BlockSpec((D_MODEL, 128), const)],
        out_specs=[pl.BlockSpec((OUT_TM, D_MODEL), rows),
                   pl.BlockSpec((OUT_TM, D_MODEL), rows),
                   pl.BlockSpec((OUT_TM, 128), rows)],
        out_shape=[jax.ShapeDtypeStruct((T_ALL, D_MODEL), F32),
                   jax.ShapeDtypeStruct((T_ALL, D_MODEL), F32),
                   jax.ShapeDtypeStruct((T_ALL, 128), F32)],
        compiler_params=_cparams("parallel"),
        name="out_projection",
    )(y_ssd, y_mla, y_hy, x, mod, nw, w, router)


def _final_norm_kernel(x_ref, w_ref, o_ref):
    x = x_ref[...]
    o_ref[...] = x * lax.rsqrt(jnp.mean(x * x, axis=-1, keepdims=True) + EPS) * w_ref[...]


def final_norm(x, w):
    tm = 512
    return pl.pallas_call(
        _final_norm_kernel,
        grid=(SEQ // tm,),
        in_specs=[pl.BlockSpec((tm, D_MODEL), lambda i: (i, 0)), pl.BlockSpec((1, D_MODEL), lambda i: (0, 0))],
        out_specs=pl.BlockSpec((tm, D_MODEL), lambda i: (i, 0)),
        out_shape=jax.ShapeDtypeStruct((SEQ, D_MODEL), F32),
        compiler_params=_cparams("parallel"),
        name="final_norm",
    )(x, w)


CONV_TM = 256
CONV_HALO = 8


def _conv_kernel(main_ref, prev_ref, next_ref, w_ref, b_ref, o_ref, ext_ref, *, taps, act):
    i = pl.program_id(0)
    n_lat = SEQ // CONV_TM
    has_prev = jnp.logical_and(i != 0, i != n_lat)
    has_next = jnp.logical_and(i != n_lat - 1, i != n_lat)
    ext_ref[0:CONV_HALO, :] = jnp.where(has_prev, prev_ref[...], 0.0)
    ext_ref[CONV_HALO:CONV_HALO + CONV_TM, :] = main_ref[...]
    ext_ref[CONV_HALO + CONV_TM:2 * CONV_HALO + CONV_TM, :] = jnp.where(has_next, next_ref[...], 0.0)
    first = CONV_HALO - taps // 2
    acc = w_ref[0:1, :] * ext_ref[first:first + CONV_TM, :] + b_ref[...]
    for k in range(1, taps):
        acc = acc + w_ref[k:k + 1, :] * ext_ref[first + k:first + k + CONV_TM, :]
    o_ref[...] = _silu(acc) if act else acc


def dwconv(p, col_block, width, w, b, act):
    taps = w.shape[0]
    r = CONV_TM // CONV_HALO
    last = T_ALL // CONV_HALO - 1
    return pl.pallas_call(
        functools.partial(_conv_kernel, taps=taps, act=act),
        grid=(T_ALL // CONV_TM,),
        in_specs=[pl.BlockSpec((CONV_TM, width), lambda i: (i, col_block)),
                  pl.BlockSpec((CONV_HALO, width), lambda i: (jnp.maximum(i * r - 1, 0), col_block)),
                  pl.BlockSpec((CONV_HALO, width), lambda i: (jnp.minimum((i + 1) * r, last), col_block)),
                  pl.BlockSpec((taps, width), lambda i: (0, 0)),
                  pl.BlockSpec((1, width), lambda i: (0, 0))],
        out_specs=pl.BlockSpec((CONV_TM, width), lambda i: (i, 0)),
        out_shape=jax.ShapeDtypeStruct((T_ALL, width), F32),
        scratch_shapes=[pltpu.VMEM((CONV_TM + 2 * CONV_HALO, width), F32)],
        compiler_params=_cparams("parallel"),
        name="dwconv",
    )(p, p, p, w, b)


N_CHUNK = T_ALL // SSD_CHUNK
N_CTX_CHUNK = CTX_LEN // SSD_CHUNK
SSD_PAIRS = SSD_HEADS // 2


def _ssd_kernel(*refs, reverse):
    if reverse:
        (xbc_ref, dt_ref, bias_ref, alog_ref, tri_ref, yf_ref, z_ref, dskip_ref, nw_ref, y_ref, s_ref) = refs
    else:
        (xbc_ref, dt_ref, bias_ref, alog_ref, tri_ref, y_ref, s_ref) = refs
    q = SSD_CHUNK

    @pl.when(pl.program_id(0) == 0)
    def _():
        s_ref[...] = jnp.zeros_like(s_ref)

    dtraw = dt_ref[...]
    if reverse:
        dtraw = pltpu.roll(dtraw, 128 - SSD_HEADS, 1)
    v = dtraw + bias_ref[...]
    dt = jnp.maximum(v, 0.0) + jnp.log(1.0 + jnp.exp(-jnp.abs(v)))
    da = dt * (-jnp.exp(alog_ref[...]))
    d1, d2, d3 = _split3(da)
    tri = tri_ref[...]
    acum = _dot(tri, d1) + _dot(tri, d2) + _dot(tri, d3)
    acum_t = acum.T
    dt_t = dt.T
    last = 0 if reverse else q - 1
    ii = lax.broadcasted_iota(jnp.int32, (q, q), 0)
    jj = lax.broadcasted_iota(jnp.int32, (q, q), 1)
    mask = (jj >= ii) if reverse else (jj <= ii)
    left = lax.broadcasted_iota(jnp.int32, (q, 128), 1) < SSD_HEAD_DIM
    left_row = lax.broadcasted_iota(jnp.int32, (1, 128), 1) < SSD_HEAD_DIM

    for g in range(SSD_GROUPS):
        bg = xbc_ref[:, SSD_WIDTH + g * SSD_STATE:SSD_WIDTH + (g + 1) * SSD_STATE]
        cg = xbc_ref[:, SSD_WIDTH + (SSD_GROUPS + g) * SSD_STATE:SSD_WIDTH + (SSD_GROUPS + g + 1) * SSD_STATE]
        bgb = bg.astype(BF16)
        cgb = cg.astype(BF16)
        cb = lax.dot_general(cgb, bgb, (((1,), (1,)), ((), ())), preferred_element_type=F32)
        bg_t = bg.T.astype(BF16)
        for pr in range(g * SSD_PAIRS // SSD_GROUPS, (g + 1) * SSD_PAIRS // SSD_GROUPS):
            sl = slice(pr * 128, (pr + 1) * 128)
            xp = xbc_ref[:, sl]
            xpb = xp.astype(BF16)
            heads = (2 * pr, 2 * pr + 1)

            def scores(h):
                seg = acum[:, h:h + 1] - acum_t[h:h + 1, :]
                decay = jnp.exp(jnp.where(mask, seg, -1e30))
                return (cb * decay * dt_t[h:h + 1, :]).astype(BF16)

            y_diag = jnp.where(left, _dot(scores(heads[0]), xpb), _dot(scores(heads[1]), xpb))
            col = [acum[:, h:h + 1] for h in heads]
            tot = [acum_t[h:h + 1, last:last + 1] for h in heads]
            sp = s_ref[:, sl]
            y_off = _dot(cgb, sp.astype(BF16)) * jnp.where(left, jnp.exp(col[0]), jnp.exp(col[1]))
            y_ref[:, sl] = y_diag + y_off
            wgt = jnp.where(left, jnp.exp(tot[0] - col[0]) * dt[:, heads[0]:heads[0] + 1],
                            jnp.exp(tot[1] - col[1]) * dt[:, heads[1]:heads[1] + 1])
            dec = jnp.where(left_row, jnp.exp(tot[0]), jnp.exp(tot[1]))
            s_ref[:, sl] = sp * dec + _dot(bg_t, (xp * wgt).astype(BF16))

    if reverse:
        y = (y_ref[...] + yf_ref[...] + dskip_ref[...] * xbc_ref[:, 0:SSD_WIDTH]) * _silu(z_ref[...])
        y_ref[...] = y * lax.rsqrt(jnp.mean(y * y, axis=-1, keepdims=True) + EPS) * nw_ref[...]


def ssd_scan(xbc, p, bias, alog, tri, reverse, yf=None, dskip=None, nw=None):
    if reverse:
        order = lambda s: N_CHUNK - 1 - s
    else:
        order = lambda s: jnp.where(s < N_CTX_CHUNK, N_CHUNK - N_CTX_CHUNK + s, s - N_CTX_CHUNK)
    const = lambda s: (0, 0)
    in_specs = [pl.BlockSpec((SSD_CHUNK, SSD_XBC), lambda s: (order(s), 0)),
                pl.BlockSpec((SSD_CHUNK, 128), lambda s: (order(s), P_DT // 128)),
                pl.BlockSpec((1, 128), const),
                pl.BlockSpec((1, 128), const),
                pl.BlockSpec((SSD_CHUNK, SSD_CHUNK), const)]
    args = [xbc, p, bias, alog, tri]
    if reverse:
        in_specs += [pl.BlockSpec((SSD_CHUNK, SSD_WIDTH), lambda s: (order(s), 0)),
                     pl.BlockSpec((SSD_CHUNK, SSD_WIDTH), lambda s: (order(s), P_Z // SSD_WIDTH)),
                     pl.BlockSpec((1, SSD_WIDTH), const),
                     pl.BlockSpec((1, SSD_WIDTH), const)]
        args += [yf, p, dskip, nw]
    return pl.pallas_call(
        functools.partial(_ssd_kernel, reverse=reverse),
        grid=(N_CHUNK,),
        in_specs=in_specs,
        out_specs=pl.BlockSpec((SSD_CHUNK, SSD_WIDTH), lambda s: (order(s), 0)),
        out_shape=jax.ShapeDtypeStruct((T_ALL, SSD_WIDTH), F32),
        scratch_shapes=[pltpu.VMEM((SSD_STATE, SSD_WIDTH), F32)],
        compiler_params=_cparams("arbitrary"),
        name="ssd_scan_bwd" if reverse else "ssd_scan_fwd",
    )(*args)


ROUTE_TB = 256
WS_ROWS = 72


def _route_kernel(aff_ref, lstrict_ref, pos_ref, ws_ref):
    lstrict = lstrict_ref[...]
    ws_ref[...] = jnp.zeros_like(ws_ref)

    def stream(row0, nrows, k, slot_base, ws_row0):
        nblk = nrows // ROUTE_TB

        def bits(b):
            r0 = pl.multiple_of(row0 + b * ROUTE_TB, ROUTE_TB)
            return r0, pltpu.bitcast(aff_ref[pl.ds(r0, ROUTE_TB), :], jnp.int32)

        def count(pred):
            def body(b, acc):
                return acc + jnp.sum(jnp.where(pred(bits(b)[1]), 1.0, 0.0), axis=0, keepdims=True)
            return lax.fori_loop(0, nblk, body, jnp.zeros((1, 128), F32))

        def search(it, thr):
            cand = thr | jnp.left_shift(jnp.int32(1), 30 - it)
            return jnp.where(count(lambda v: v >= cand) >= k, cand, thr)

        thr = lax.fori_loop(0, 31, search, jnp.zeros((1, 128), jnp.int32))
        need = k - count(lambda v: v > thr)

        def place(b, carry):
            ties_before, picked_before = carry
            r0, v = bits(b)
            gt = v > thr
            eq = v == thr
            eqf = jnp.where(eq, 1.0, 0.0)
            tie_rank = _dot(lstrict, eqf.astype(BF16)) + ties_before
            sel = jnp.logical_or(gt, jnp.logical_and(eq, tie_rank < need))
            self_ = jnp.where(sel, 1.0, 0.0)
            rank = _dot(lstrict, self_.astype(BF16)) + picked_before
            pos_ref[pl.ds(r0, ROUTE_TB), :] = jnp.where(sel, rank + slot_base, -1.0).astype(jnp.int32)
            w = ws_row0 + 2 * b
            ws_ref[pl.ds(w, 1), :] = (picked_before + slot_base).astype(jnp.int32)
            ws_ref[pl.ds(w + 1, 1), :] = (rank[128:129, :] + slot_base).astype(jnp.int32)
            return (ties_before + jnp.sum(eqf, axis=0, keepdims=True),
                    picked_before + jnp.sum(self_, axis=0, keepdims=True))

        zero = jnp.zeros((1, 128), F32)
        lax.fori_loop(0, nblk, place, (zero, zero))

    stream(0, SEQ, CAP_L, 0.0, 0)
    stream(SEQ, CTX_LEN, CAP_C, float(CAP_L), SEQ // 128)


def route(aff, lstrict):
    return pl.pallas_call(
        _route_kernel,
        out_shape=[jax.ShapeDtypeStruct((T_ALL, 128), jnp.int32),
                   jax.ShapeDtypeStruct((WS_ROWS, 128), jnp.int32)],
        compiler_params=pltpu.CompilerParams(vmem_limit_bytes=VMEM_LIMIT),
        name="route",
    )(aff, lstrict)


ACC_ROWS = 1280


def _compact_kernel(ws_ref, pos_ref, tv_ref, acc_ref):
    acc_ref[...] = jnp.zeros_like(acc_ref)
    tv = tv_ref[...]
    lane = lax.broadcasted_iota(jnp.int32, (1, 128), 1)
    srow = lax.broadcasted_iota(jnp.int32, (256, 1), 0)

    def body(tb, carry):
        r0 = pl.multiple_of(tb * 128, 128)
        slot_of_token = pos_ref[pl.ds(r0, 128), :].astype(F32).T
        scale = jnp.where(lane == 0, 1.0, jnp.where(lane == 1, jnp.asarray(tb * 128, F32), 0.0))
        for e in range(N_EXPERTS):
            base = pl.multiple_of(jnp.minimum(ws_ref[tb, e] & -128, ACC_ROWS - 256), 128)
            onehot = jnp.where(slot_of_token[e:e + 1, :] == (srow + base).astype(F32), 1.0, 0.0)
            found = _dot(onehot.astype(BF16), tv)
            acc_ref[e, pl.ds(base, 256), :] += found * scale
        return carry

    lax.fori_loop(0, N_TB, body, 0)


def compact(ws, pos, tv):
    return pl.pallas_call(
        _compact_kernel,
        grid_spec=pltpu.PrefetchScalarGridSpec(
            num_scalar_prefetch=1,
            grid=(1,),
            in_specs=[pl.BlockSpec((T_ALL, 128), lambda i, ws: (0, 0)),
                      pl.BlockSpec((128, 128), lambda i, ws: (0, 0))],
            out_specs=pl.BlockSpec((N_EXPERTS, ACC_ROWS, 128), lambda i, ws: (0, 0, 0))),
        out_shape=jax.ShapeDtypeStruct((N_EXPERTS, ACC_ROWS, 128), F32),
        compiler_params=_cparams("arbitrary"),
        name="compact",
    )(ws, pos, tv)


FFN_TF = 256
GATHER_ROWS = 128
N_GATHER = SLOTS // GATHER_ROWS
GATHER_PLAN = ((0, 1, 2), (3, 4), (5, 6), (7, 8))
GATHER_SLOTS = max(len(c) for c in GATHER_PLAN)
assert len(GATHER_PLAN) == EXPERT_FF // FFN_TF and sum(len(c) for c in GATHER_PLAN) == N_GATHER


def _ffn_kernel(idx_ref, h_ref, w1_ref, w3_ref, w2_ref, y_ref, x_ref, acc_ref, gbuf_ref, sem_ref):
    e = pl.program_id(0)
    f = pl.program_id(1)
    par = e % 2

    def row_copy(slot, r, token):
        return pltpu.make_async_copy(h_ref.at[pl.ds(token, 1), :], gbuf_ref.at[slot, pl.ds(r, 1), :],
                                     sem_ref.at[slot])

    def issue(expert, chunks):
        for slot, chunk in enumerate(chunks):
            def body(r, carry):
                row_copy(slot, r, idx_ref[expert, chunk * GATHER_ROWS + r]).start()
                return carry
            lax.fori_loop(0, GATHER_ROWS, body, 0, unroll=8)

    def drain(buf, chunks):
        for slot, chunk in enumerate(chunks):
            def body(r, carry):
                row_copy(slot, r, 0).wait()
                return carry
            lax.fori_loop(0, GATHER_ROWS, body, 0, unroll=8)
            x_ref[buf, chunk * GATHER_ROWS:(chunk + 1) * GATHER_ROWS, :] = gbuf_ref[slot].astype(BF16)

    @pl.when(jnp.logical_and(e == 0, f == 0))
    def _():
        for chunks in GATHER_PLAN:
            issue(0, chunks)
            drain(0, chunks)

    prefetch = e + 1 < N_EXPERTS
    for step, chunks in enumerate(GATHER_PLAN):
        @pl.when(jnp.logical_and(prefetch, f == step))
        def _():
            issue(e + 1, chunks)

    x = x_ref[par]
    a = _dot(x, w1_ref[...].astype(BF16))
    b = _dot(x, w3_ref[...].astype(BF16))
    hid = (_silu(a) * b).astype(BF16)
    part = _dot(hid, w2_ref[...].astype(BF16))

    @pl.when(f == 0)
    def _():
        acc_ref[...] = part

    @pl.when(f > 0)
    def _():
        acc_ref[...] += part

    @pl.when(f == EXPERT_FF // FFN_TF - 1)
    def _():
        y_ref[...] = acc_ref[...].astype(BF16)

    for step, chunks in enumerate(GATHER_PLAN):
        @pl.when(jnp.logical_and(prefetch, f == step))
        def _():
            drain(1 - par, chunks)


def expert_ffn(idx, h2, w1, w3, w2, layer):
    return pl.pallas_call(
        _ffn_kernel,
        grid_spec=pltpu.PrefetchScalarGridSpec(
            num_scalar_prefetch=1,
            grid=(N_EXPERTS, EXPERT_FF // FFN_TF),
            in_specs=[pl.BlockSpec(memory_space=pl.ANY),
                      pl.BlockSpec((None, None, D_MODEL, FFN_TF), lambda e, f, idx: (layer, e, 0, f)),
                      pl.BlockSpec((None, None, D_MODEL, FFN_TF), lambda e, f, idx: (layer, e, 0, f)),
                      pl.BlockSpec((None, None, FFN_TF, D_MODEL), lambda e, f, idx: (layer, e, f, 0))],
            out_specs=pl.BlockSpec((None, SLOTS, D_MODEL), lambda e, f, idx: (e, 0, 0)),
            scratch_shapes=[pltpu.VMEM((2, SLOTS, D_MODEL), BF16),
                            pltpu.VMEM((SLOTS, D_MODEL), F32),
                            pltpu.VMEM((GATHER_SLOTS, GATHER_ROWS, D_MODEL), F32),
                            pltpu.SemaphoreType.DMA((GATHER_SLOTS,))]),
        out_shape=jax.ShapeDtypeStruct((N_EXPERTS, SLOTS, D_MODEL), BF16),
        compiler_params=_cparams("arbitrary", "arbitrary"),
        name="expert_ffn",
    )(idx, h2, w1, w3, w2)


CMB_TD = 512
CMB_WIN = 256


def _combine_kernel(ws_ref, y_ref, pos_ref, aff_ref, x1_ref, g_ref, o_ref):
    tb = pl.program_id(1)
    pos = pos_ref[...]
    aff = aff_ref[...]
    lane = lax.broadcasted_iota(jnp.int32, (1, CMB_WIN), 1)
    acc = jnp.zeros((128, CMB_TD), F32)
    for e in range(N_EXPERTS):
        base = pl.multiple_of(jnp.minimum(ws_ref[tb, e] & -128, SLOTS - CMB_WIN), 128)
        onehot = jnp.where(pos[:, e:e + 1] == lane + base, 1.0, 0.0).astype(BF16)
        acc = acc + aff[:, e:e + 1] * _dot(onehot, y_ref[e, pl.ds(base, CMB_WIN), :])
    g = jnp.where(tb >= SEQ // 128, g_ref[1:2, :], g_ref[0:1, :])
    o_ref[...] = x1_ref[...] + g * acc


def combine(ws, y, pos, aff, x1, mod):
    nd = D_MODEL // CMB_TD
    return pl.pallas_call(
        _combine_kernel,
        grid_spec=pltpu.PrefetchScalarGridSpec(
            num_scalar_prefetch=1,
            grid=(nd, N_TB),
            in_specs=[pl.BlockSpec((N_EXPERTS, SLOTS, CMB_TD), lambda d, t, ws: (0, 0, d)),
                      pl.BlockSpec((128, 128), lambda d, t, ws: (t, 0)),
                      pl.BlockSpec((128, 128), lambda d, t, ws: (t, 0)),
                      pl.BlockSpec((128, CMB_TD), lambda d, t, ws: (t, d)),
                      pl.BlockSpec((8, CMB_TD), lambda d, t, ws: (0, 5 * nd + d))],
            out_specs=pl.BlockSpec((128, CMB_TD), lambda d, t, ws: (t, d))),
        out_shape=jax.ShapeDtypeStruct((T_ALL, D_MODEL), F32),
        compiler_params=_cparams("arbitrary", "arbitrary"),
        name="combine",
    )(ws, y, pos, aff, x1, mod)


HF_TM = 256
HY_COLS = HY_ORDER * HY_WIDTH


def _split2(a):
    hi = a.astype(BF16)
    return hi, (a - hi.astype(F32)).astype(BF16)


def _dot3(a, b):
    return _dot(a[0], b[0]) + (_dot(a[0], b[1]) + _dot(a[1], b[0]))


def _hy_filter_kernel(w1_ref, b1_ref, w2_ref, b2_ref, w3_ref, b3_ref, wo_ref, fr_ref, bands_ref, deltas_ref,
                      k_ref, sum_ref, *, seq):
    i = pl.program_id(0)
    n = i * HF_TM + lax.broadcasted_iota(jnp.int32, (HF_TM, 1), 0)
    lag = jnp.where(n < seq, n, jnp.where(n == seq, 0, 2 * seq - n)).astype(F32)
    t = lag / (seq - 1.0)
    ang = (2.0 * math.pi * lag) * bands_ref[...] / float(seq)
    lane = lax.broadcasted_iota(jnp.int32, (1, 128), 1)
    feats = jnp.where(lane == 0, t, jnp.where(lane <= HY_BANDS, jnp.cos(ang),
                                               jnp.where(lane <= 2 * HY_BANDS, -jnp.sin(ang), 0.0)))
    fr = fr_ref[...]
    hdn = jnp.sin(fr * (_dot_hp(feats, w1_ref[...]) + b1_ref[...]))
    hdn = jnp.sin(fr * (_dot_hp(hdn, w2_ref[...]) + b2_ref[...]))
    hdn = jnp.sin(fr * (_dot_hp(hdn, w3_ref[...]) + b3_ref[...]))
    h = _dot_hp(hdn, wo_ref[...]) * jnp.exp(-t * deltas_ref[...])

    @pl.when(i == 0)
    def _():
        sum_ref[...] = jnp.zeros_like(sum_ref)

    sum_ref[0:1, :] += jnp.sum(jnp.abs(h), axis=0, keepdims=True)
    k_ref[...] = jnp.where(n == seq, 0.0, h)


def hy_filter(seq, w1, b1, w2, b2, w3, b3, wo, fr, bands, deltas):
    half = seq // HF_TM
    const = lambda i: (0, 0)
    sq = pl.BlockSpec((128, 128), const)
    row = pl.BlockSpec((1, 128), const)
    return pl.pallas_call(
        functools.partial(_hy_filter_kernel, seq=seq),
        grid=(2 * half,),
        in_specs=[sq, row, sq, row, sq, row,
                  pl.BlockSpec((128, HY_COLS), lambda i: (0, jnp.where(i >= half, 1, 0))),
                  row, row, pl.BlockSpec((1, HY_COLS), const)],
        out_specs=[pl.BlockSpec((HF_TM, HY_COLS), lambda i: (i, 0)),
                   pl.BlockSpec((8, HY_COLS), const)],
        out_shape=[jax.ShapeDtypeStruct((2 * seq, HY_COLS), F32),
                   jax.ShapeDtypeStruct((8, HY_COLS), F32)],
        compiler_params=_cparams("arbitrary"),
        name="hy_filter",
    )(w1, b1, w2, b2, w3, b3, wo, fr, bands, deltas)


FFT_R = 128
FFT_N = FFT_R * FFT_R
DFT_TN = 4096


def _dft1_kernel(x_ref, w_ref, re_ref, im_ref):
    res = _dot3(_split2(w_ref[...]), _split2(x_ref[...]))
    re_ref[...] = res[0:FFT_R]
    im_ref[...] = res[FFT_R:2 * FFT_R]


def dft_rows(x2d, wmat):
    k, m = x2d.shape
    return pl.pallas_call(
        _dft1_kernel,
        grid=(m // DFT_TN,),
        in_specs=[pl.BlockSpec((k, DFT_TN), lambda j: (0, j)),
                  pl.BlockSpec((2 * FFT_R, k), lambda j: (0, 0))],
        out_specs=[pl.BlockSpec((FFT_R, DFT_TN), lambda j: (0, j))] * 2,
        out_shape=[jax.ShapeDtypeStruct((FFT_R, m), F32)] * 2,
        compiler_params=_cparams("parallel"),
        name="hy_dft_rows",
    )(x2d, wmat)


def _twiddle(re, im, tc, ts, conj):
    if conj:
        return re * tc - im * ts, im * tc + re * ts
    return re * tc + im * ts, im * tc - re * ts


def _cdft(wp, re, im):
    res = _dot3(wp, _split2(jnp.concatenate([re, im], axis=0)))
    return res[0:FFT_R], res[FFT_R:2 * FFT_R]


def _mid_filter_kernel(are_ref, aim_ref, twc_ref, tws_ref, wf_ref, wi_ref, kre_ref, kim_ref):
    re, im = _twiddle(are_ref[0], aim_ref[0], twc_ref[0], tws_ref[0], False)
    kre_ref[0], kim_ref[0] = _cdft(_split2(wf_ref[...]), re, im)


def _mid_data_kernel(are_ref, aim_ref, kre_ref, kim_ref, twc_ref, tws_ref, wf_ref, wi_ref, bre_ref, bim_ref):
    tc = twc_ref[0]
    ts = tws_ref[0]
    re, im = _twiddle(are_ref[0], aim_ref[0], tc, ts, False)
    xr, xi = _cdft(_split2(wf_ref[...]), re, im)
    kr = kre_ref[0]
    ki = kim_ref[0]
    br, bi = _cdft(_split2(wi_ref[...]), xr * kr - xi * ki, xr * ki + xi * kr)
    bre_ref[0], bim_ref[0] = _twiddle(br, bi, tc, ts, True)


def fft_mid(are, aim, twc, tws, cmat, smat, kf=None, order=0):
    width = are.shape[-1]
    slab = pl.BlockSpec((1, FFT_R, width), lambda k1: (k1, 0, 0))
    tw = pl.BlockSpec((1, FFT_R, 1), lambda k1: (k1, 0, 0))
    sq = pl.BlockSpec((2 * FFT_R, 2 * FFT_R), lambda k1: (0, 0))
    if kf is None:
        body, ins, specs = _mid_filter_kernel, (are, aim, twc, tws, cmat, smat), [slab, slab, tw, tw, sq, sq]
    else:
        kslab = pl.BlockSpec((1, FFT_R, width), lambda k1: (k1, 0, order))
        body, ins = _mid_data_kernel, (are, aim, kf[0], kf[1], twc, tws, cmat, smat)
        specs = [slab, slab, kslab, kslab, tw, tw, sq, sq]
    return pl.pallas_call(
        body,
        grid=(FFT_R,),
        in_specs=specs,
        out_specs=[slab, slab],
        out_shape=[jax.ShapeDtypeStruct((FFT_R, FFT_R, width), F32)] * 2,
        compiler_params=_cparams("parallel"),
        name="hy_fft_mid_filter" if kf is None else "hy_fft_mid",
    )(*ins)


def _idft1_kernel(bre_ref, bim_ref, w_ref, gate_ref, zin_ref, bias_ref, sum_ref, o_ref):
    b = jnp.concatenate([bre_ref[...], bim_ref[...]], axis=0)
    conv = _dot3(_split2(w_ref[...]), _split2(b))
    o_ref[...] = gate_ref[...] * (conv / (FFT_N * sum_ref[...]) + bias_ref[...] * zin_ref[...])


def idft_rows_gated(bre, bim, wmat, gate, zin, bias_t, sum_t):
    rows, m = gate.shape
    const = lambda j: (0, 0)
    tile = lambda r: pl.BlockSpec((r, DFT_TN), lambda j: (0, j))
    return pl.pallas_call(
        _idft1_kernel,
        grid=(m // DFT_TN,),
        in_specs=[tile(FFT_R), tile(FFT_R), pl.BlockSpec((rows, 2 * FFT_R), const),
                  tile(rows), tile(rows), pl.BlockSpec((1, DFT_TN), const), pl.BlockSpec((1, DFT_TN), const)],
        out_specs=tile(rows),
        out_shape=jax.ShapeDtypeStruct((rows, m), F32),
        compiler_params=_cparams("parallel"),
        name="hy_idft_rows",
    )(bre, bim, wmat, gate, zin, bias_t, sum_t)


CTX_N = 2 * CTX_LEN


def _hy_ctx_kernel(u_ref, k_ref, c_ref, s_ref, gate_ref, bias_ref, sum_ref, o_ref):
    cp = _split2(c_ref[...])
    sp = _split2(s_ref[...])
    head = lambda pair: (pair[0][:, 0:CTX_LEN], pair[1][:, 0:CTX_LEN])
    top = lambda pair: (pair[0][0:CTX_LEN, :], pair[1][0:CTX_LEN, :])
    u = u_ref[...]
    up = _split2(u)
    ur = _dot3(head(cp), up)
    ui = -_dot3(head(sp), up)
    kp = _split2(k_ref[...])
    kr = _dot3(cp, kp)
    ki = -_dot3(sp, kp)
    conv = (_dot3(top(cp), _split2(ur * kr - ui * ki)) - _dot3(top(sp), _split2(ur * ki + ui * kr)))
    o_ref[...] = gate_ref[...] * (conv / (CTX_N * sum_ref[...]) + bias_ref[...] * u)


def hy_ctx_conv(u, kfilt, order, cmat, smat, gate, bias, sums):
    const = lambda i: (0, 0)
    chan = pl.BlockSpec((CTX_LEN, HY_WIDTH), const)
    vec = pl.BlockSpec((1, HY_WIDTH), const)
    return pl.pallas_call(
        _hy_ctx_kernel,
        grid=(1,),
        in_specs=[chan, pl.BlockSpec((CTX_N, HY_WIDTH), lambda i: (0, order)),
                  pl.BlockSpec((CTX_N, CTX_N), const), pl.BlockSpec((CTX_N, CTX_N), const), chan, vec, vec],
        out_specs=chan,
        out_shape=jax.ShapeDtypeStruct((CTX_LEN, HY_WIDTH), F32),
        compiler_params=_cparams("arbitrary"),
        name="hy_ctx_conv",
    )(u, kfilt, cmat, smat, gate, bias, sums)


def _rope_tables():
    rows = SEQ // GRID_W
    r, col = jnp.meshgrid(jnp.arange(rows), jnp.arange(GRID_W), indexing='ij')
    per_axis = MLA_ROPE // 2
    inv = ROPE_THETA ** (-jnp.arange(0, per_axis, 2, dtype=F32) / per_axis)
    ang = jnp.concatenate([r.reshape(-1, 1) * inv, col.reshape(-1, 1) * inv], axis=-1)
    cos = jnp.concatenate([jnp.cos(ang), jnp.ones((CTX_LEN, per_axis), F32)], axis=0)
    sin = jnp.concatenate([jnp.sin(ang), jnp.zeros((CTX_LEN, per_axis), F32)], axis=0)
    z = jnp.zeros_like(cos)
    return jnp.concatenate([cos, z, cos, z], axis=-1), jnp.concatenate([-sin, z, sin, z], axis=-1)


def _permute_w_in(w_in):
    z, xbc, dt, cq, ckv, kr, hy = jnp.split(w_in, np.cumsum(
        (SSD_WIDTH, SSD_XBC, 2 * SSD_HEADS, MLA_Q_RANK, MLA_KV_RANK, MLA_ROPE)).tolist(), axis=-1)
    zeros = lambda n: jnp.zeros(w_in.shape[:-1] + (n,), w_in.dtype)
    return jnp.concatenate([hy, z, ckv, xbc, kr[..., :32], zeros(32), kr[..., 32:], zeros(32),
                            dt, zeros(128 - 2 * SSD_HEADS), cq], axis=-1).astype(BF16)


def _pad_w_uq(w_uq):
    w = w_uq.reshape(DEPTH, MLA_Q_RANK, MLA_HEADS, MLA_QK)
    zeros = jnp.zeros(w.shape[:-1] + (32,), w.dtype)
    w = jnp.concatenate([w[..., :MLA_NOPE], w[..., MLA_NOPE:MLA_NOPE + 32], zeros,
                         w[..., MLA_NOPE + 32:], zeros], axis=-1)
    return w.reshape(DEPTH, MLA_Q_RANK, MLA_HEADS * QK_PAD).astype(BF16)


def kernel(x, c, ctx, c_ctx, w_mod, b_mod, norm1_w, norm2_w, w_in, w_out, ssd_conv_w, ssd_conv_b,
           ssd_dt_bias, ssd_a_log, ssd_d, ssd_norm_w, mla_q_norm_w, mla_w_uq, mla_kv_norm_w, mla_w_ukv,
           hy_short_w, hy_short_b, hy_w1, hy_b1, hy_w2, hy_b2, hy_w3, hy_b3, hy_w_out, hy_freq, hy_bias,
           moe_router, moe_w1, moe_w3, moe_w2, final_norm_w):
    xj = jnp.concatenate([x[0], ctx[0]], axis=0)
    cvec = jnp.concatenate([c, c_ctx[None, :], jnp.zeros((6, D_MODEL), F32)], axis=0)
    mods = modulation(cvec, w_mod, b_mod)
    c2, s2 = _rope_tables()
    w_in_p = _permute_w_in(w_in)
    w_uq_p = _pad_w_uq(mla_w_uq)
    w_ukv_b = mla_w_ukv.astype(BF16)
    w_out_b = w_out.astype(BF16)
    router_p = jnp.pad(moe_router, ((0, 0), (0, 0), (0, 128 - N_EXPERTS)))
    lane_pad = lambda a: jnp.pad(a, ((0, 0), (0, 0), (0, 128 - SSD_HEADS)))[:, :, None, :]
    dt_bias = lane_pad(ssd_dt_bias)
    a_log = lane_pad(ssd_a_log)
    d_skip = jnp.repeat(ssd_d, SSD_HEAD_DIM, axis=-1)[:, None, :]
    tri_lower = jnp.asarray(np.tril(np.ones((SSD_CHUNK, SSD_CHUNK), np.float32)), BF16)
    tri_upper = jnp.asarray(np.triu(np.ones((SSD_CHUNK, SSD_CHUNK), np.float32)), BF16)
    lstrict = jnp.asarray(np.tril(np.ones((ROUTE_TB, ROUTE_TB), np.float32), -1), BF16)
    tv = np.zeros((128, 128), np.float32)
    tv[:, 0] = np.arange(128)
    tv[:, 1] = 1.0
    tvals = jnp.asarray(tv, BF16)

    pad_to = lambda a, r, c_: jnp.pad(a, ((0, 0), (0, r - a.shape[1]), (0, c_ - a.shape[2])))
    hf_w1, hf_w2, hf_w3 = pad_to(hy_w1, 128, 128), pad_to(hy_w2, 128, 128), pad_to(hy_w3, 128, 128)
    hf_b1, hf_b2, hf_b3, hf_fr = (pad_to(a[:, None, :], 1, 128) for a in (hy_b1, hy_b2, hy_b3, hy_freq))
    hf_wo = hy_w_out.reshape(DEPTH, HY_FFN, HY_ORDER, 2, HY_WIDTH).transpose(0, 1, 3, 2, 4)
    hf_wo = pad_to(hf_wo.reshape(DEPTH, HY_FFN, 2 * HY_COLS), 128, 2 * HY_COLS)
    band_vals = jnp.linspace(1e-4, HY_BANDS - 1, HY_BANDS, dtype=F32)
    bands = jnp.concatenate([jnp.zeros((1,), F32), band_vals, band_vals,
                             jnp.zeros((128 - HY_EMB,), F32)])[None, :]
    deltas = jnp.abs(jnp.linspace(math.log(HY_TARGET) / HY_SLOW_DECAY_PCT,
                                  math.log(HY_TARGET) / HY_FAST_DECAY_PCT, HY_WIDTH, dtype=F32))
    deltas = jnp.tile(deltas, HY_ORDER)[None, :]
    jk = np.outer(np.arange(FFT_R), np.arange(FFT_R))
    cos_r = np.cos(2 * np.pi * (jk % FFT_R) / FFT_R)
    sin_r = np.sin(2 * np.pi * (jk % FFT_R) / FFT_R)
    half = FFT_R // 2
    dft_in_full = jnp.asarray(np.concatenate([cos_r, -sin_r], axis=0), F32)
    dft_in_half = jnp.asarray(np.concatenate([cos_r[:, :half], -sin_r[:, :half]], axis=0), F32)
    dft_fwd = jnp.asarray(np.block([[cos_r, sin_r], [-sin_r, cos_r]]), F32)
    dft_inv = jnp.asarray(np.block([[cos_r, -sin_r], [sin_r, cos_r]]), F32)
    dft_out_half = jnp.asarray(np.concatenate([cos_r[:half], -sin_r[:half]], axis=1), F32)
    tw_cos = jnp.asarray(np.cos(2 * np.pi * jk / FFT_N), F32)[:, :, None]
    tw_sin = jnp.asarray(np.sin(2 * np.pi * jk / FFT_N), F32)[:, :, None]
    jk_c = np.outer(np.arange(CTX_N), np.arange(CTX_N)) % CTX_N
    cos_c = jnp.asarray(np.cos(2 * np.pi * jk_c / CTX_N), F32)
    sin_c = jnp.asarray(np.sin(2 * np.pi * jk_c / CTX_N), F32)

    for i in range(DEPTH):
        mod = mods[i]
        p = in_projection(xj, mod, norm1_w[i][None, :], w_in_p[i])
        q, k, v = mla_projection(p, c2, s2, mla_q_norm_w[i][None, :], mla_kv_norm_w[i][None, :],
                                 w_uq_p[i], w_ukv_b[i])
        y_mla = attention(q, k, v)

        xbc = dwconv(p, P_XBC // SSD_XBC, SSD_XBC, ssd_conv_w[i], ssd_conv_b[i][None, :], act=True)
        y_f = ssd_scan(xbc, p, dt_bias[i, 0], a_log[i, 0], tri_lower, reverse=False)
        y_ssd = ssd_scan(xbc, p, dt_bias[i, 1], a_log[i, 1], tri_upper, reverse=True, yf=y_f,
                         dskip=d_skip[i], nw=ssd_norm_w[i][None, :])

        vxx = [dwconv(p, j, HY_WIDTH, hy_short_w[i][:, j * HY_WIDTH:(j + 1) * HY_WIDTH],
                      hy_short_b[i][None, j * HY_WIDTH:(j + 1) * HY_WIDTH], act=False) for j in range(3)]
        filt = (hf_w1[i], hf_b1[i], hf_w2[i], hf_b2[i], hf_w3[i], hf_b3[i], hf_wo[i], hf_fr[i], bands, deltas)
        k_l, sum_l = hy_filter(SEQ, *filt)
        k_c, sum_c = hy_filter(CTX_LEN, *filt)
        are, aim = dft_rows(k_l.reshape(FFT_R, FFT_R * HY_COLS), dft_in_full)
        kf = fft_mid(are.reshape(FFT_R, FFT_R, HY_COLS), aim.reshape(FFT_R, FFT_R, HY_COLS),
                     tw_cos, tw_sin, dft_fwd, dft_inv)
        z_l = vxx[0][:SEQ].reshape(half, FFT_R * HY_WIDTH)
        z_c = vxx[0][SEQ:]
        for o in range(HY_ORDER):
            ch = slice(o * HY_WIDTH, (o + 1) * HY_WIDTH)
            are, aim = dft_rows(z_l, dft_in_half)
            bre, bim = fft_mid(are.reshape(FFT_R, FFT_R, HY_WIDTH), aim.reshape(FFT_R, FFT_R, HY_WIDTH),
                               tw_cos, tw_sin, dft_fwd, dft_inv, kf=kf, order=o)
            z_l = idft_rows_gated(bre.reshape(FFT_R, FFT_R * HY_WIDTH), bim.reshape(FFT_R, FFT_R * HY_WIDTH),
                                  dft_out_half,
                                  vxx[1 + o][:SEQ].reshape(half, FFT_R * HY_WIDTH), z_l,
                                  jnp.tile(hy_bias[i, o][None, :], (1, DFT_TN // HY_WIDTH)),
                                  jnp.tile(sum_l[0:1, ch], (1, DFT_TN // HY_WIDTH)))
            z_c = hy_ctx_conv(z_c, k_c, o, cos_c, sin_c, vxx[1 + o][SEQ:], hy_bias[i, o][None, :], sum_c[0:1, ch])
        y_hy = jnp.concatenate([z_l.reshape(SEQ, HY_WIDTH), z_c], axis=0)

        x1, h2, aff = out_projection(y_ssd, y_mla, y_hy, xj, mod, norm2_w[i][None, :], w_out_b[i], router_p[i])
        pos, ws = route(aff, lstrict)
        found = compact(ws, pos, tvals)
        idx = (found[:, :SLOTS, 0] + found[:, :SLOTS, 1]).astype(jnp.int32)
        y_moe = expert_ffn(idx, h2, moe_w1, moe_w3, moe_w2, i)
        xj = combine(ws, y_moe, pos, aff, x1, mod)
    return final_norm(xj, final_norm_w[None, :])[None]
```

```python
import functools
import math

import jax
import jax.numpy as jnp
import numpy as np
from jax import lax
from jax.experimental import pallas as pl
from jax.experimental.pallas import tpu as pltpu

D_MODEL = 2048
SEQ = 8192
DEPTH = 4
GRID_W = 64
CTX_LEN = 256
T_ALL = SEQ + CTX_LEN
EPS = 1e-6
N_MOD = 6

SSD_HEAD_DIM = 64
SSD_WIDTH = 768
SSD_HEADS = 12
SSD_GROUPS = 2
SSD_STATE = 128
SSD_CONV = 5
SSD_CHUNK = 128
SSD_XBC = SSD_WIDTH + 2 * SSD_GROUPS * SSD_STATE
MLA_NOPE = 128
MLA_ROPE = 64
MLA_V = 128
MLA_QK = MLA_NOPE + MLA_ROPE
MLA_WIDTH = 768
MLA_HEADS = 6
MLA_Q_RANK = 512
MLA_KV_RANK = 256
ROPE_THETA = 10000.0
HY_WIDTH = 512
HY_ORDER = 2
HY_SHORT = 3
HY_BANDS = 16
HY_EMB = 1 + 2 * HY_BANDS
HY_FFN = 64
HY_TARGET = 1e-2
HY_FAST_DECAY_PCT = 0.3
HY_SLOW_DECAY_PCT = 1.5
N_EXPERTS = 16
EXPERT_FF = 1024
CAPACITY_FACTOR = 2

P_HY = 0
P_Z = 1536
P_CKV = 2304
P_XBC = 2560
P_KR = 3840
P_DT = 3968
P_CQ = 4096
P_COLS = 4608
QK_PAD = 256
V_PAD = 256
ATT_SCALE = 1.0 / math.sqrt(MLA_QK)
CAP_L = CAPACITY_FACTOR * SEQ // N_EXPERTS
CAP_C = CAPACITY_FACTOR * CTX_LEN // N_EXPERTS
SLOTS = 1152
N_TB = T_ALL // 128

F32 = jnp.float32
BF16 = jnp.bfloat16
VMEM_LIMIT = 56 * 1024 * 1024


def _cparams(*sem):
    return pltpu.CompilerParams(dimension_semantics=sem, vmem_limit_bytes=VMEM_LIMIT)


def _split3(a):
    hi = a.astype(BF16)
    r = a - hi.astype(F32)
    mid = r.astype(BF16)
    lo = (r - mid.astype(F32)).astype(BF16)
    return hi, mid, lo


def _dot(a, b):
    return jnp.dot(a, b, preferred_element_type=F32)


def _dot_hp(a, b):
    a1, a2, a3 = _split3(a)
    b1, b2, b3 = _split3(b)
    return (_dot(a1, b1) + (_dot(a1, b2) + _dot(a2, b1))
            + (_dot(a2, b2) + _dot(a1, b3) + _dot(a3, b1)))


def _silu(v):
    return v * (1.0 / (1.0 + jnp.exp(-v)))


def _mod_kernel(c_ref, w_ref, b_ref, o_ref):
    s = _silu(c_ref[...])
    o_ref[0] = _dot_hp(s, w_ref[0]) + b_ref[0]


def modulation(cvec, w_mod, b_mod):
    tn = 1024
    return pl.pallas_call(
        _mod_kernel,
        grid=(DEPTH, N_MOD * D_MODEL // tn),
        in_specs=[pl.BlockSpec((8, D_MODEL), lambda l, j: (0, 0)),
                  pl.BlockSpec((1, D_MODEL, tn), lambda l, j: (l, 0, j)),
                  pl.BlockSpec((1, 1, tn), lambda l, j: (l, 0, j))],
        out_specs=pl.BlockSpec((1, 8, tn), lambda l, j: (l, 0, j)),
        out_shape=jax.ShapeDtypeStruct((DEPTH, 8, N_MOD * D_MODEL), F32),
        compiler_params=_cparams("parallel", "parallel"),
        name="modulation",
    )(cvec, w_mod, b_mod.reshape(DEPTH, 1, N_MOD * D_MODEL))


def _norm_mod(x, nw, mod_ref, row0, k_shift, k_scale):
    n = x.shape[0]
    xn = x * lax.rsqrt(jnp.mean(x * x, axis=-1, keepdims=True) + EPS) * nw
    is_ctx = (row0 + lax.broadcasted_iota(jnp.int32, (n, 1), 0)) >= SEQ
    sl_shift = slice(k_shift * D_MODEL, (k_shift + 1) * D_MODEL)
    sl_scale = slice(k_scale * D_MODEL, (k_scale + 1) * D_MODEL)
    shift = jnp.where(is_ctx, mod_ref[1:2, sl_shift], mod_ref[0:1, sl_shift])
    scale = jnp.where(is_ctx, mod_ref[1:2, sl_scale], mod_ref[0:1, sl_scale])
    return xn * (1.0 + scale) + shift


IN_TM = 768
IN_TN = 768


def _inproj_kernel(x_ref, mod_ref, nw_ref, w_ref, o_ref, h_ref):
    @pl.when(pl.program_id(1) == 0)
    def _():
        h = _norm_mod(x_ref[...], nw_ref[...], mod_ref, pl.program_id(0) * IN_TM, 0, 1)
        h_ref[...] = h.astype(BF16)

    o_ref[...] = _dot(h_ref[...], w_ref[...])


def in_projection(x, mod, nw, w):
    return pl.pallas_call(
        _inproj_kernel,
        grid=(T_ALL // IN_TM, P_COLS // IN_TN),
        in_specs=[pl.BlockSpec((IN_TM, D_MODEL), lambda i, j: (i, 0)),
                  pl.BlockSpec((8, N_MOD * D_MODEL), lambda i, j: (0, 0)),
                  pl.BlockSpec((1, D_MODEL), lambda i, j: (0, 0)),
                  pl.BlockSpec((D_MODEL, IN_TN), lambda i, j: (0, j))],
        out_specs=pl.BlockSpec((IN_TM, IN_TN), lambda i, j: (i, j)),
        out_shape=jax.ShapeDtypeStruct((T_ALL, P_COLS), F32),
        scratch_shapes=[pltpu.VMEM((IN_TM, D_MODEL), BF16)],
        compiler_params=_cparams("parallel", "arbitrary"),
        name="in_projection",
    )(x, mod, nw, w)


MLA_TM = 768


def _rope(v, c2, s2):
    return v * c2 + pltpu.roll(v, 64, 1) * s2


def _mla_proj_kernel(cq_ref, ckv_ref, kr_ref, c2_ref, s2_ref, qnw_ref, kvnw_ref, wq_ref, wkv_ref,
                     q_ref, k_ref, v_ref):
    c2 = c2_ref[...]
    s2 = s2_ref[...]
    cq = cq_ref[...]
    cqn = cq * lax.rsqrt(jnp.mean(cq * cq, axis=-1, keepdims=True) + EPS) * qnw_ref[...]
    q = _dot(cqn.astype(BF16), wq_ref[...]) * ATT_SCALE
    ckv = ckv_ref[...]
    ckvn = ckv * lax.rsqrt(jnp.mean(ckv * ckv, axis=-1, keepdims=True) + EPS) * kvnw_ref[...]
    kv = _dot(ckvn.astype(BF16), wkv_ref[...])
    kr = _rope(kr_ref[...], c2, s2).astype(BF16)
    for h in range(MLA_HEADS):
        b = h * QK_PAD
        q_ref[h, :, 0:MLA_NOPE] = q[:, b:b + MLA_NOPE].astype(BF16)
        q_ref[h, :, MLA_NOPE:QK_PAD] = _rope(q[:, b + MLA_NOPE:b + QK_PAD], c2, s2).astype(BF16)
        k_ref[h, :, 0:MLA_NOPE] = kv[:, b:b + MLA_NOPE].astype(BF16)
        k_ref[h, :, MLA_NOPE:QK_PAD] = kr
        v_ref[h, :, 0:MLA_V] = kv[:, b + MLA_NOPE:b + QK_PAD].astype(BF16)
        v_ref[h, :, MLA_V:V_PAD] = jnp.ones((MLA_TM, V_PAD - MLA_V), BF16)


def mla_projection(p, c2, s2, qnw, kvnw, wq, wkv):
    rows = lambda i: (i, 0)
    const = lambda i: (0, 0)
    return pl.pallas_call(
        _mla_proj_kernel,
        grid=(T_ALL // MLA_TM,),
        in_specs=[pl.BlockSpec((MLA_TM, MLA_Q_RANK), lambda i: (i, P_CQ // MLA_Q_RANK)),
                  pl.BlockSpec((MLA_TM, MLA_KV_RANK), lambda i: (i, P_CKV // MLA_KV_RANK)),
                  pl.BlockSpec((MLA_TM, 128), lambda i: (i, P_KR // 128)),
                  pl.BlockSpec((MLA_TM, 128), rows),
                  pl.BlockSpec((MLA_TM, 128), rows),
                  pl.BlockSpec((1, MLA_Q_RANK), const),
                  pl.BlockSpec((1, MLA_KV_RANK), const),
                  pl.BlockSpec((MLA_Q_RANK, MLA_HEADS * QK_PAD), const),
                  pl.BlockSpec((MLA_KV_RANK, MLA_HEADS * QK_PAD), const)],
        out_specs=[pl.BlockSpec((MLA_HEADS, MLA_TM, QK_PAD), lambda i: (0, i, 0)),
                   pl.BlockSpec((MLA_HEADS, MLA_TM, QK_PAD), lambda i: (0, i, 0)),
                   pl.BlockSpec((MLA_HEADS, MLA_TM, V_PAD), lambda i: (0, i, 0))],
        out_shape=[jax.ShapeDtypeStruct((MLA_HEADS, T_ALL, QK_PAD), BF16),
                   jax.ShapeDtypeStruct((MLA_HEADS, T_ALL, QK_PAD), BF16),
                   jax.ShapeDtypeStruct((MLA_HEADS, T_ALL, V_PAD), BF16)],
        compiler_params=_cparams("parallel"),
        name="mla_projection",
    )(p, p, p, c2, s2, qnw, kvnw, wq, wkv)


ATT_TQ = 1024
ATT_TK = 512


def _attn_scores(q, k):
    return lax.dot_general(q, k, (((1,), (1,)), ((), ())), preferred_element_type=F32)


def _attn_update(s, v, m, acc):
    m_new = jnp.maximum(m, jnp.max(s, axis=-1, keepdims=True))
    alpha = jnp.exp(m - m_new)
    p = jnp.exp(s - m_new).astype(BF16)
    return m_new, alpha * acc + _dot(p, v)


def _attn_finish(acc, o_ref):
    o_ref[...] = acc[:, 0:MLA_V] / acc[:, MLA_V:MLA_V + 1]


def _attention_latent_kernel(q_ref, k_ref, v_ref, o_ref, s_ref):
    q = q_ref[0]
    n = SEQ // ATT_TK

    def rows(c):
        return pl.ds(pl.multiple_of(c * ATT_TK, ATT_TK), ATT_TK)

    def body(c2, carry):
        m, acc = carry
        c = 2 * c2
        s_ref[1] = _attn_scores(q, k_ref[0, rows(c + 1), :])
        m, acc = _attn_update(s_ref[0], v_ref[0, rows(c), :], m, acc)
        s_ref[0] = _attn_scores(q, k_ref[0, rows(c + 2), :])
        return _attn_update(s_ref[1], v_ref[0, rows(c + 1), :], m, acc)

    s_ref[0] = _attn_scores(q, k_ref[0, 0:ATT_TK, :])
    init = (jnp.full((ATT_TQ, 1), -1e30, F32), jnp.zeros((ATT_TQ, V_PAD), F32))
    m, acc = lax.fori_loop(0, n // 2 - 1, body, init)
    s_ref[1] = _attn_scores(q, k_ref[0, (n - 1) * ATT_TK:n * ATT_TK, :])
    s_ctx = _attn_scores(q, k_ref[0, SEQ:T_ALL, :])
    m, acc = _attn_update(s_ref[0], v_ref[0, (n - 2) * ATT_TK:(n - 1) * ATT_TK, :], m, acc)
    m, acc = _attn_update(s_ref[1], v_ref[0, (n - 1) * ATT_TK:n * ATT_TK, :], m, acc)
    m, acc = _attn_update(s_ctx, v_ref[0, SEQ:T_ALL, :], m, acc)
    _attn_finish(acc, o_ref)


def _attention_context_kernel(q_ref, k_ref, v_ref, o_ref):
    init = (jnp.full((CTX_LEN, 1), -1e30, F32), jnp.zeros((CTX_LEN, V_PAD), F32))
    m, acc = _attn_update(_attn_scores(q_ref[0], k_ref[0]), v_ref[0], *init)
    _attn_finish(acc, o_ref)


def attention(q, k, v):
    y_l = pl.pallas_call(
        _attention_latent_kernel,
        grid=(MLA_HEADS, SEQ // ATT_TQ),
        in_specs=[pl.BlockSpec((1, ATT_TQ, QK_PAD), lambda h, i: (h, i, 0)),
                  pl.BlockSpec((1, T_ALL, QK_PAD), lambda h, i: (h, 0, 0)),
                  pl.BlockSpec((1, T_ALL, V_PAD), lambda h, i: (h, 0, 0))],
        out_specs=pl.BlockSpec((ATT_TQ, MLA_V), lambda h, i: (i, h)),
        out_shape=jax.ShapeDtypeStruct((SEQ, MLA_WIDTH), F32),
        scratch_shapes=[pltpu.VMEM((2, ATT_TQ, ATT_TK), F32)],
        compiler_params=_cparams("parallel", "parallel"),
        name="attention_latent",
    )(q, k, v)
    cblk = SEQ // CTX_LEN
    y_c = pl.pallas_call(
        _attention_context_kernel,
        grid=(MLA_HEADS,),
        in_specs=[pl.BlockSpec((1, CTX_LEN, QK_PAD), lambda h: (h, cblk, 0)),
                  pl.BlockSpec((1, CTX_LEN, QK_PAD), lambda h: (h, cblk, 0)),
                  pl.BlockSpec((1, CTX_LEN, V_PAD), lambda h: (h, cblk, 0))],
        out_specs=pl.BlockSpec((CTX_LEN, MLA_V), lambda h: (0, h)),
        out_shape=jax.ShapeDtypeStruct((CTX_LEN, MLA_WIDTH), F32),
        compiler_params=_cparams("parallel"),
        name="attention_context",
    )(q, k, v)
    return jnp.concatenate([y_l, y_c], axis=0)


OUT_TM = 384


def _outproj_kernel(ys_ref, ym_ref, yh_ref, x_ref, mod_ref, nw_ref, w_ref, r_ref, x1_ref, h2_ref, aff_ref):
    y = _dot(ys_ref[...].astype(BF16), w_ref[0:SSD_WIDTH, :])
    y += _dot(ym_ref[...].astype(BF16), w_ref[SSD_WIDTH:SSD_WIDTH + MLA_WIDTH, :])
    y += _dot(yh_ref[...].astype(BF16), w_ref[SSD_WIDTH + MLA_WIDTH:D_MODEL, :])
    row0 = pl.program_id(0) * OUT_TM
    is_ctx = (row0 + lax.broadcasted_iota(jnp.int32, (OUT_TM, 1), 0)) >= SEQ
    gate = jnp.where(is_ctx, mod_ref[1:2, 2 * D_MODEL:3 * D_MODEL], mod_ref[0:1, 2 * D_MODEL:3 * D_MODEL])
    x1 = x_ref[...] + gate * y
    x1_ref[...] = x1
    h2 = _norm_mod(x1, nw_ref[...], mod_ref, row0, 3, 4)
    h2_ref[...] = h2
    logits = _dot3(_split2(h2), _split2(r_ref[...]))
    lane = lax.broadcasted_iota(jnp.int32, logits.shape, 1)
    logits = jnp.where(lane < N_EXPERTS, logits, -1e30)
    e = jnp.exp(logits - jnp.max(logits, axis=-1, keepdims=True))
    aff_ref[...] = e / jnp.sum(e, axis=-1, keepdims=True)


def out_projection(y_ssd, y_mla, y_hy, x, mod, nw, w, router):
    rows = lambda i: (i, 0)
    const = lambda i: (0, 0)
    return pl.pallas_call(
        _outproj_kernel,
        grid=(T_ALL // OUT_TM,),
        in_specs=[pl.BlockSpec((OUT_TM, SSD_WIDTH), rows),
                  pl.BlockSpec((OUT_TM, MLA_WIDTH), rows),
                  pl.BlockSpec((OUT_TM, HY_WIDTH), rows),
                  pl.BlockSpec((OUT_TM, D_MODEL), rows),
                  pl.BlockSpec((8, N_MOD * D_MODEL), const),
                  pl.BlockSpec((1, D_MODEL), const),
                  pl.BlockSpec((D_MODEL, D_MODEL), const),
                  pl.BlockSpec((D_MODEL, 128), const)],
        out_specs=[pl.BlockSpec((OUT_TM, D_MODEL), rows),
                   pl.BlockSpec((OUT_TM, D_MODEL), rows),
                   pl.BlockSpec((OUT_TM, 128), rows)],
        out_shape=[jax.ShapeDtypeStruct((T_ALL, D_MODEL), F32),
                   jax.ShapeDtypeStruct((T_ALL, D_MODEL), F32),
                   jax.ShapeDtypeStruct((T_ALL, 128), F32)],
        compiler_params=_cparams("parallel"),
        name="out_projection",
    )(y_ssd, y_mla, y_hy, x, mod, nw, w, router)


def _final_norm_kernel(x_ref, w_ref, o_ref):
    x = x_ref[...]
    o_ref[...] = x * lax.rsqrt(jnp.mean(x * x, axis=-1, keepdims=True) + EPS) * w_ref[...]


def final_norm(x, w):
    tm = 512
    return pl.pallas_call(
        _final_norm_kernel,
        grid=(SEQ // tm,),
        in_specs=[pl.BlockSpec((tm, D_MODEL), lambda i: (i, 0)), pl.BlockSpec((1, D_MODEL), lambda i: (0, 0))],
        out_specs=pl.BlockSpec((tm, D_MODEL), lambda i: (i, 0)),
        out_shape=jax.ShapeDtypeStruct((SEQ, D_MODEL), F32),
        compiler_params=_cparams("parallel"),
        name="final_norm",
    )(x, w)


CONV_TM = 256
CONV_HALO = 8


def _conv_kernel(main_ref, prev_ref, next_ref, w_ref, b_ref, o_ref, ext_ref, *, taps, act):
    i = pl.program_id(0)
    n_lat = SEQ // CONV_TM
    has_prev = jnp.logical_and(i != 0, i != n_lat)
    has_next = jnp.logical_and(i != n_lat - 1, i != n_lat)
    ext_ref[0:CONV_HALO, :] = jnp.where(has_prev, prev_ref[...], 0.0)
    ext_ref[CONV_HALO:CONV_HALO + CONV_TM, :] = main_ref[...]
    ext_ref[CONV_HALO + CONV_TM:2 * CONV_HALO + CONV_TM, :] = jnp.where(has_next, next_ref[...], 0.0)
    first = CONV_HALO - taps // 2
    acc = w_ref[0:1, :] * ext_ref[first:first + CONV_TM, :] + b_ref[...]
    for k in range(1, taps):
        acc = acc + w_ref[k:k + 1, :] * ext_ref[first + k:first + k + CONV_TM, :]
    o_ref[...] = _silu(acc) if act else acc


def dwconv(p, col_block, width, w, b, act):
    taps = w.shape[0]
    r = CONV_TM // CONV_HALO
    last = T_ALL // CONV_HALO - 1
    return pl.pallas_call(
        functools.partial(_conv_kernel, taps=taps, act=act),
        grid=(T_ALL // CONV_TM,),
        in_specs=[pl.BlockSpec((CONV_TM, width), lambda i: (i, col_block)),
                  pl.BlockSpec((CONV_HALO, width), lambda i: (jnp.maximum(i * r - 1, 0), col_block)),
                  pl.BlockSpec((CONV_HALO, width), lambda i: (jnp.minimum((i + 1) * r, last), col_block)),
                  pl.BlockSpec((taps, width), lambda i: (0, 0)),
                  pl.BlockSpec((1, width), lambda i: (0, 0))],
        out_specs=pl.BlockSpec((CONV_TM, width), lambda i: (i, 0)),
        out_shape=jax.ShapeDtypeStruct((T_ALL, width), F32),
        scratch_shapes=[pltpu.VMEM((CONV_TM + 2 * CONV_HALO, width), F32)],
        compiler_params=_cparams("parallel"),
        name="dwconv",
    )(p, p, p, w, b)


N_CHUNK = T_ALL // SSD_CHUNK
N_CTX_CHUNK = CTX_LEN // SSD_CHUNK
SSD_PAIRS = SSD_HEADS // 2


def _ssd_kernel(*refs, reverse):
    if reverse:
        (xbc_ref, dt_ref, bias_ref, alog_ref, tri_ref, yf_ref, z_ref, dskip_ref, nw_ref, y_ref, s_ref) = refs
    else:
        (xbc_ref, dt_ref, bias_ref, alog_ref, tri_ref, y_ref, s_ref) = refs
    q = SSD_CHUNK

    @pl.when(pl.program_id(0) == 0)
    def _():
        s_ref[...] = jnp.zeros_like(s_ref)

    dtraw = dt_ref[...]
    if reverse:
        dtraw = pltpu.roll(dtraw, 128 - SSD_HEADS, 1)
    v = dtraw + bias_ref[...]
    dt = jnp.maximum(v, 0.0) + jnp.log(1.0 + jnp.exp(-jnp.abs(v)))
    da = dt * (-jnp.exp(alog_ref[...]))
    d1, d2, d3 = _split3(da)
    tri = tri_ref[...]
    acum = _dot(tri, d1) + _dot(tri, d2) + _dot(tri, d3)
    acum_t = acum.T
    dt_t = dt.T
    last = 0 if reverse else q - 1
    ii = lax.broadcasted_iota(jnp.int32, (q, q), 0)
    jj = lax.broadcasted_iota(jnp.int32, (q, q), 1)
    mask = (jj >= ii) if reverse else (jj <= ii)
    left = lax.broadcasted_iota(jnp.int32, (q, 128), 1) < SSD_HEAD_DIM
    left_row = lax.broadcasted_iota(jnp.int32, (1, 128), 1) < SSD_HEAD_DIM

    for g in range(SSD_GROUPS):
        bg = xbc_ref[:, SSD_WIDTH + g * SSD_STATE:SSD_WIDTH + (g + 1) * SSD_STATE]
        cg = xbc_ref[:, SSD_WIDTH + (SSD_GROUPS + g) * SSD_STATE:SSD_WIDTH + (SSD_GROUPS + g + 1) * SSD_STATE]
        bgb = bg.astype(BF16)
        cgb = cg.astype(BF16)
        cb = lax.dot_general(cgb, bgb, (((1,), (1,)), ((), ())), preferred_element_type=F32)
        bg_t = bg.T.astype(BF16)
        for pr in range(g * SSD_PAIRS // SSD_GROUPS, (g + 1) * SSD_PAIRS // SSD_GROUPS):
            sl = slice(pr * 128, (pr + 1) * 128)
            xp = xbc_ref[:, sl]
            xpb = xp.astype(BF16)
            heads = (2 * pr, 2 * pr + 1)

            def scores(h):
                seg = acum[:, h:h + 1] - acum_t[h:h + 1, :]
                decay = jnp.exp(jnp.where(mask, seg, -1e30))
                return (cb * decay * dt_t[h:h + 1, :]).astype(BF16)

            y_diag = jnp.where(left, _dot(scores(heads[0]), xpb), _dot(scores(heads[1]), xpb))
            col = [acum[:, h:h + 1] for h in heads]
            tot = [acum_t[h:h + 1, last:last + 1] for h in heads]
            sp = s_ref[:, sl]
            y_off = _dot(cgb, sp.astype(BF16)) * jnp.where(left, jnp.exp(col[0]), jnp.exp(col[1]))
            y_ref[:, sl] = y_diag + y_off
            wgt = jnp.where(left, jnp.exp(tot[0] - col[0]) * dt[:, heads[0]:heads[0] + 1],
                            jnp.exp(tot[1] - col[1]) * dt[:, heads[1]:heads[1] + 1])
            dec = jnp.where(left_row, jnp.exp(tot[0]), jnp.exp(tot[1]))
            s_ref[:, sl] = sp * dec + _dot(bg_t, (xp * wgt).astype(BF16))

    if reverse:
        y = (y_ref[...] + yf_ref[...] + dskip_ref[...] * xbc_ref[:, 0:SSD_WIDTH]) * _silu(z_ref[...])
        y_ref[...] = y * lax.rsqrt(jnp.mean(y * y, axis=-1, keepdims=True) + EPS) * nw_ref[...]


def ssd_scan(xbc, p, bias, alog, tri, reverse, yf=None, dskip=None, nw=None):
    if reverse:
        order = lambda s: N_CHUNK - 1 - s
    else:
        order = lambda s: jnp.where(s < N_CTX_CHUNK, N_CHUNK - N_CTX_CHUNK + s, s - N_CTX_CHUNK)
    const = lambda s: (0, 0)
    in_specs = [pl.BlockSpec((SSD_CHUNK, SSD_XBC), lambda s: (order(s), 0)),
                pl.BlockSpec((SSD_CHUNK, 128), lambda s: (order(s), P_DT // 128)),
                pl.BlockSpec((1, 128), const),
                pl.BlockSpec((1, 128), const),
                pl.BlockSpec((SSD_CHUNK, SSD_CHUNK), const)]
    args = [xbc, p, bias, alog, tri]
    if reverse:
        in_specs += [pl.BlockSpec((SSD_CHUNK, SSD_WIDTH), lambda s: (order(s), 0)),
                     pl.BlockSpec((SSD_CHUNK, SSD_WIDTH), lambda s: (order(s), P_Z // SSD_WIDTH)),
                     pl.BlockSpec((1, SSD_WIDTH), const),
                     pl.BlockSpec((1, SSD_WIDTH), const)]
        args += [yf, p, dskip, nw]
    return pl.pallas_call(
        functools.partial(_ssd_kernel, reverse=reverse),
        grid=(N_CHUNK,),
        in_specs=in_specs,
        out_specs=pl.BlockSpec((SSD_CHUNK, SSD_WIDTH), lambda s: (order(s), 0)),
        out_shape=jax.ShapeDtypeStruct((T_ALL, SSD_WIDTH), F32),
        scratch_shapes=[pltpu.VMEM((SSD_STATE, SSD_WIDTH), F32)],
        compiler_params=_cparams("arbitrary"),
        name="ssd_scan_bwd" if reverse else "ssd_scan_fwd",
    )(*args)


ROUTE_TB = 256
WS_ROWS = 72


def _route_kernel(aff_ref, lstrict_ref, pos_ref, ws_ref):
    lstrict = lstrict_ref[...]
    ws_ref[...] = jnp.zeros_like(ws_ref)

    def stream(row0, nrows, k, slot_base, ws_row0):
        nblk = nrows // ROUTE_TB

        def bits(b):
            r0 = pl.multiple_of(row0 + b * ROUTE_TB, ROUTE_TB)
            return r0, pltpu.bitcast(aff_ref[pl.ds(r0, ROUTE_TB), :], jnp.int32)

        def count(pred):
            def body(b, acc):
                return acc + jnp.sum(jnp.where(pred(bits(b)[1]), 1.0, 0.0), axis=0, keepdims=True)
            return lax.fori_loop(0, nblk, body, jnp.zeros((1, 128), F32))

        def search(it, thr):
            cand = thr | jnp.left_shift(jnp.int32(1), 30 - it)
            return jnp.where(count(lambda v: v >= cand) >= k, cand, thr)

        thr = lax.fori_loop(0, 31, search, jnp.zeros((1, 128), jnp.int32))
        need = k - count(lambda v: v > thr)

        def place(b, carry):
            ties_before, picked_before = carry
            r0, v = bits(b)
            gt = v > thr
            eq = v == thr
            eqf = jnp.where(eq, 1.0, 0.0)
            tie_rank = _dot(lstrict, eqf.astype(BF16)) + ties_before
            sel = jnp.logical_or(gt, jnp.logical_and(eq, tie_rank < need))
            self_ = jnp.where(sel, 1.0, 0.0)
            rank = _dot(lstrict, self_.astype(BF16)) + picked_before
            pos_ref[pl.ds(r0, ROUTE_TB), :] = jnp.where(sel, rank + slot_base, -1.0).astype(jnp.int32)
            w = ws_row0 + 2 * b
            ws_ref[pl.ds(w, 1), :] = (picked_before + slot_base).astype(jnp.int32)
            ws_ref[pl.ds(w + 1, 1), :] = (rank[128:129, :] + slot_base).astype(jnp.int32)
            return (ties_before + jnp.sum(eqf, axis=0, keepdims=True),
                    picked_before + jnp.sum(self_, axis=0, keepdims=True))

        zero = jnp.zeros((1, 128), F32)
        lax.fori_loop(0, nblk, place, (zero, zero))

    stream(0, SEQ, CAP_L, 0.0, 0)
    stream(SEQ, CTX_LEN, CAP_C, float(CAP_L), SEQ // 128)


def route(aff, lstrict):
    return pl.pallas_call(
        _route_kernel,
        out_shape=[jax.ShapeDtypeStruct((T_ALL, 128), jnp.int32),
                   jax.ShapeDtypeStruct((WS_ROWS, 128), jnp.int32)],
        compiler_params=pltpu.CompilerParams(vmem_limit_bytes=VMEM_LIMIT),
        name="route",
    )(aff, lstrict)


ACC_ROWS = 1280


def _compact_kernel(ws_ref, pos_ref, tv_ref, acc_ref):
    acc_ref[...] = jnp.zeros_like(acc_ref)
    tv = tv_ref[...]
    lane = lax.broadcasted_iota(jnp.int32, (1, 128), 1)
    srow = lax.broadcasted_iota(jnp.int32, (256, 1), 0)

    def body(tb, carry):
        r0 = pl.multiple_of(tb * 128, 128)
        slot_of_token = pos_ref[pl.ds(r0, 128), :].astype(F32).T
        scale = jnp.where(lane == 0, 1.0, jnp.where(lane == 1, jnp.asarray(tb * 128, F32), 0.0))
        for e in range(N_EXPERTS):
            base = pl.multiple_of(jnp.minimum(ws_ref[tb, e] & -128, ACC_ROWS - 256), 128)
            onehot = jnp.where(slot_of_token[e:e + 1, :] == (srow + base).astype(F32), 1.0, 0.0)
            found = _dot(onehot.astype(BF16), tv)
            acc_ref[e, pl.ds(base, 256), :] += found * scale
        return carry

    lax.fori_loop(0, N_TB, body, 0)


def compact(ws, pos, tv):
    return pl.pallas_call(
        _compact_kernel,
        grid_spec=pltpu.PrefetchScalarGridSpec(
            num_scalar_prefetch=1,
            grid=(1,),
            in_specs=[pl.BlockSpec((T_ALL, 128), lambda i, ws: (0, 0)),
                      pl.BlockSpec((128, 128), lambda i, ws: (0, 0))],
            out_specs=pl.BlockSpec((N_EXPERTS, ACC_ROWS, 128), lambda i, ws: (0, 0, 0))),
        out_shape=jax.ShapeDtypeStruct((N_EXPERTS, ACC_ROWS, 128), F32),
        compiler_params=_cparams("arbitrary"),
        name="compact",
    )(ws, pos, tv)


FFN_TF = 256
GATHER_ROWS = 128
N_GATHER = SLOTS // GATHER_ROWS
GATHER_PLAN = ((0, 1, 2), (3, 4), (5, 6), (7, 8))
GATHER_SLOTS = max(len(c) for c in GATHER_PLAN)
assert len(GATHER_PLAN) == EXPERT_FF // FFN_TF and sum(len(c) for c in GATHER_PLAN) == N_GATHER


def _ffn_kernel(idx_ref, h_ref, w1_ref, w3_ref, w2_ref, y_ref, x_ref, acc_ref, gbuf_ref, sem_ref):
    e = pl.program_id(0)
    f = pl.program_id(1)
    par = e % 2

    def row_copy(slot, r, token):
        return pltpu.make_async_copy(h_ref.at[pl.ds(token, 1), :], gbuf_ref.at[slot, pl.ds(r, 1), :],
                                     sem_ref.at[slot])

    def issue(expert, chunks):
        for slot, chunk in enumerate(chunks):
            def body(r, carry):
                row_copy(slot, r, idx_ref[expert, chunk * GATHER_ROWS + r]).start()
                return carry
            lax.fori_loop(0, GATHER_ROWS, body, 0, unroll=8)

    def drain(buf, chunks):
        for slot, chunk in enumerate(chunks):
            def body(r, carry):
                row_copy(slot, r, 0).wait()
                return carry
            lax.fori_loop(0, GATHER_ROWS, body, 0, unroll=8)
            x_ref[buf, chunk * GATHER_ROWS:(chunk + 1) * GATHER_ROWS, :] = gbuf_ref[slot].astype(BF16)

    @pl.when(jnp.logical_and(e == 0, f == 0))
    def _():
        for chunks in GATHER_PLAN:
            issue(0, chunks)
            drain(0, chunks)

    prefetch = e + 1 < N_EXPERTS
    for step, chunks in enumerate(GATHER_PLAN):
        @pl.when(jnp.logical_and(prefetch, f == step))
        def _():
            issue(e + 1, chunks)

    x = x_ref[par]
    a = _dot(x, w1_ref[...].astype(BF16))
    b = _dot(x, w3_ref[...].astype(BF16))
    hid = (_silu(a) * b).astype(BF16)
    part = _dot(hid, w2_ref[...].astype(BF16))

    @pl.when(f == 0)
    def _():
        acc_ref[...] = part

    @pl.when(f > 0)
    def _():
        acc_ref[...] += part

    @pl.when(f == EXPERT_FF // FFN_TF - 1)
    def _():
        y_ref[...] = acc_ref[...].astype(BF16)

    for step, chunks in enumerate(GATHER_PLAN):
        @pl.when(jnp.logical_and(prefetch, f == step))
        def _():
            drain(1 - par, chunks)


def expert_ffn(idx, h2, w1, w3, w2, layer):
    return pl.pallas_call(
        _ffn_kernel,
        grid_spec=pltpu.PrefetchScalarGridSpec(
            num_scalar_prefetch=1,
            grid=(N_EXPERTS, EXPERT_FF // FFN_TF),
            in_specs=[pl.BlockSpec(memory_space=pl.ANY),
                      pl.BlockSpec((None, None, D_MODEL, FFN_TF), lambda e, f, idx: (layer, e, 0, f)),
                      pl.BlockSpec((None, None, D_MODEL, FFN_TF), lambda e, f, idx: (layer, e, 0, f)),
                      pl.BlockSpec((None, None, FFN_TF, D_MODEL), lambda e, f, idx: (layer, e, f, 0))],
            out_specs=pl.BlockSpec((None, SLOTS, D_MODEL), lambda e, f, idx: (e, 0, 0)),
            scratch_shapes=[pltpu.VMEM((2, SLOTS, D_MODEL), BF16),
                            pltpu.VMEM((SLOTS, D_MODEL), F32),
                            pltpu.VMEM((GATHER_SLOTS, GATHER_ROWS, D_MODEL), F32),
                            pltpu.SemaphoreType.DMA((GATHER_SLOTS,))]),
        out_shape=jax.ShapeDtypeStruct((N_EXPERTS, SLOTS, D_MODEL), BF16),
        compiler_params=_cparams("arbitrary", "arbitrary"),
        name="expert_ffn",
    )(idx, h2, w1, w3, w2)


CMB_TD = 512
CMB_WIN = 256


def _combine_kernel(ws_ref, y_ref, pos_ref, aff_ref, x1_ref, g_ref, o_ref):
    tb = pl.program_id(1)
    pos = pos_ref[...]
    aff = aff_ref[...]
    lane = lax.broadcasted_iota(jnp.int32, (1, CMB_WIN), 1)
    acc = jnp.zeros((128, CMB_TD), F32)
    for e in range(N_EXPERTS):
        base = pl.multiple_of(jnp.minimum(ws_ref[tb, e] & -128, SLOTS - CMB_WIN), 128)
        onehot = jnp.where(pos[:, e:e + 1] == lane + base, 1.0, 0.0).astype(BF16)
        acc = acc + aff[:, e:e + 1] * _dot(onehot, y_ref[e, pl.ds(base, CMB_WIN), :])
    g = jnp.where(tb >= SEQ // 128, g_ref[1:2, :], g_ref[0:1, :])
    o_ref[...] = x1_ref[...] + g * acc


def combine(ws, y, pos, aff, x1, mod):
    nd = D_MODEL // CMB_TD
    return pl.pallas_call(
        _combine_kernel,
        grid_spec=pltpu.PrefetchScalarGridSpec(
            num_scalar_prefetch=1,
            grid=(nd, N_TB),
            in_specs=[pl.BlockSpec((N_EXPERTS, SLOTS, CMB_TD), lambda d, t, ws: (0, 0, d)),
                      pl.BlockSpec((128, 128), lambda d, t, ws: (t, 0)),
                      pl.BlockSpec((128, 128), lambda d, t, ws: (t, 0)),
                      pl.BlockSpec((128, CMB_TD), lambda d, t, ws: (t, d)),
                      pl.BlockSpec((8, CMB_TD), lambda d, t, ws: (0, 5 * nd + d))],
            out_specs=pl.BlockSpec((128, CMB_TD), lambda d, t, ws: (t, d))),
        out_shape=jax.ShapeDtypeStruct((T_ALL, D_MODEL), F32),
        compiler_params=_cparams("arbitrary", "arbitrary"),
        name="combine",
    )(ws, y, pos, aff, x1, mod)


HF_TM = 256
HY_COLS = HY_ORDER * HY_WIDTH


def _split2(a):
    hi = a.astype(BF16)
    return hi, (a - hi.astype(F32)).astype(BF16)


def _dot3(a, b):
    return _dot(a[0], b[0]) + (_dot(a[0], b[1]) + _dot(a[1], b[0]))


def _hy_filter_kernel(w1_ref, b1_ref, w2_ref, b2_ref, w3_ref, b3_ref, wo_ref, fr_ref, bands_ref, deltas_ref,
                      k_ref, sum_ref, *, seq):
    i = pl.program_id(0)
    n = i * HF_TM + lax.broadcasted_iota(jnp.int32, (HF_TM, 1), 0)
    lag = jnp.where(n < seq, n, jnp.where(n == seq, 0, 2 * seq - n)).astype(F32)
    t = lag / (seq - 1.0)
    ang = (2.0 * math.pi * lag) * bands_ref[...] / float(seq)
    lane = lax.broadcasted_iota(jnp.int32, (1, 128), 1)
    phase = jnp.where(lane > HY_BANDS, 0.5 * math.pi, 0.0)
    feats = jnp.where(lane == 0, t, jnp.where(lane <= 2 * HY_BANDS, jnp.cos(ang + phase), 0.0))
    fr = fr_ref[...]
    mm = lambda a, w_ref: _dot3(_split2(a), _split2(w_ref[...]))
    hdn = jnp.sin(fr * (mm(feats, w1_ref) + b1_ref[...]))
    hdn = jnp.sin(fr * (mm(hdn, w2_ref) + b2_ref[...]))
    hdn = jnp.sin(fr * (mm(hdn, w3_ref) + b3_ref[...]))
    h = mm(hdn, wo_ref) * jnp.exp(-t * deltas_ref[...])

    @pl.when(i == 0)
    def _():
        sum_ref[...] = jnp.zeros_like(sum_ref)

    sum_ref[0:1, :] += jnp.sum(jnp.abs(h), axis=0, keepdims=True)
    k_ref[...] = jnp.where(n == seq, 0.0, h)


def hy_filter(seq, w1, b1, w2, b2, w3, b3, wo, fr, bands, deltas):
    half = seq // HF_TM
    const = lambda i: (0, 0)
    sq = pl.BlockSpec((128, 128), const)
    row = pl.BlockSpec((1, 128), const)
    return pl.pallas_call(
        functools.partial(_hy_filter_kernel, seq=seq),
        grid=(2 * half,),
        in_specs=[sq, row, sq, row, sq, row,
                  pl.BlockSpec((128, HY_COLS), lambda i: (0, jnp.where(i >= half, 1, 0))),
                  row, row, pl.BlockSpec((1, HY_COLS), const)],
        out_specs=[pl.BlockSpec((HF_TM, HY_COLS), lambda i: (i, 0)),
                   pl.BlockSpec((8, HY_COLS), const)],
        out_shape=[jax.ShapeDtypeStruct((2 * seq, HY_COLS), F32),
                   jax.ShapeDtypeStruct((8, HY_COLS), F32)],
        compiler_params=_cparams("arbitrary"),
        name="hy_filter",
    )(w1, b1, w2, b2, w3, b3, wo, fr, bands, deltas)


FFT_R = 128
FFT_N = FFT_R * FFT_R
DFT_TN = 4096


def _dft1_kernel(x_ref, w_ref, re_ref, im_ref):
    res = _dot3(_split2(w_ref[...]), _split2(x_ref[...]))
    re_ref[...] = res[0:FFT_R]
    im_ref[...] = res[FFT_R:2 * FFT_R]


def dft_rows(x2d, wmat):
    k, m = x2d.shape
    return pl.pallas_call(
        _dft1_kernel,
        grid=(m // DFT_TN,),
        in_specs=[pl.BlockSpec((k, DFT_TN), lambda j: (0, j)),
                  pl.BlockSpec((2 * FFT_R, k), lambda j: (0, 0))],
        out_specs=[pl.BlockSpec((FFT_R, DFT_TN), lambda j: (0, j))] * 2,
        out_shape=[jax.ShapeDtypeStruct((FFT_R, m), F32)] * 2,
        compiler_params=_cparams("parallel"),
        name="hy_dft_rows",
    )(x2d, wmat)


def _twiddle(re, im, tc, ts, conj):
    if conj:
        return re * tc - im * ts, im * tc + re * ts
    return re * tc + im * ts, im * tc - re * ts


def _cdft(wp, re, im):
    res = _dot3(wp, _split2(jnp.concatenate([re, im], axis=0)))
    return res[0:FFT_R], res[FFT_R:2 * FFT_R]


FFT_G = 2


def _mid_filter_kernel(are_ref, aim_ref, twc_ref, tws_ref, wf_ref, wi_ref, kre_ref, kim_ref):
    wf = _split2(wf_ref[...])
    for g in range(FFT_G):
        re, im = _twiddle(are_ref[g], aim_ref[g], twc_ref[g], tws_ref[g], False)
        kre_ref[g], kim_ref[g] = _cdft(wf, re, im)


def _mid_data_kernel(are_ref, aim_ref, kre_ref, kim_ref, twc_ref, tws_ref, wf_ref, wi_ref, bre_ref, bim_ref):
    wf = _split2(wf_ref[...])
    wi = _split2(wi_ref[...])
    for g in range(FFT_G):
        tc = twc_ref[g]
        ts = tws_ref[g]
        re, im = _twiddle(are_ref[g], aim_ref[g], tc, ts, False)
        xr, xi = _cdft(wf, re, im)
        kr = kre_ref[g]
        ki = kim_ref[g]
        br, bi = _cdft(wi, xr * kr - xi * ki, xr * ki + xi * kr)
        bre_ref[g], bim_ref[g] = _twiddle(br, bi, tc, ts, True)


def fft_mid(are, aim, twc, tws, cmat, smat, kf=None, order=0):
    width = are.shape[-1]
    slab = pl.BlockSpec((FFT_G, FFT_R, width), lambda k1: (k1, 0, 0))
    tw = pl.BlockSpec((FFT_G, FFT_R, 1), lambda k1: (k1, 0, 0))
    sq = pl.BlockSpec((2 * FFT_R, 2 * FFT_R), lambda k1: (0, 0))
    if kf is None:
        body, ins, specs = _mid_filter_kernel, (are, aim, twc, tws, cmat, smat), [slab, slab, tw, tw, sq, sq]
    else:
        kslab = pl.BlockSpec((FFT_G, FFT_R, width), lambda k1: (k1, 0, order))
        body, ins = _mid_data_kernel, (are, aim, kf[0], kf[1], twc, tws, cmat, smat)
        specs = [slab, slab, kslab, kslab, tw, tw, sq, sq]
    return pl.pallas_call(
        body,
        grid=(FFT_R // FFT_G,),
        in_specs=specs,
        out_specs=[slab, slab],
        out_shape=[jax.ShapeDtypeStruct((FFT_R, FFT_R, width), F32)] * 2,
        compiler_params=_cparams("parallel"),
        name="hy_fft_mid_filter" if kf is None else "hy_fft_mid",
    )(*ins)


def _idft1_kernel(bre_ref, bim_ref, w_ref, gate_ref, zin_ref, bias_ref, sum_ref, o_ref):
    b = jnp.concatenate([bre_ref[...], bim_ref[...]], axis=0)
    conv = _dot3(_split2(w_ref[...]), _split2(b))
    o_ref[...] = gate_ref[...] * (conv / (FFT_N * sum_ref[...]) + bias_ref[...] * zin_ref[...])


def idft_rows_gated(bre, bim, wmat, gate, zin, bias_t, sum_t):
    rows, m = gate.shape
    const = lambda j: (0, 0)
    tile = lambda r: pl.BlockSpec((r, DFT_TN), lambda j: (0, j))
    return pl.pallas_call(
        _idft1_kernel,
        grid=(m // DFT_TN,),
        in_specs=[tile(FFT_R), tile(FFT_R), pl.BlockSpec((rows, 2 * FFT_R), const),
                  tile(rows), tile(rows), pl.BlockSpec((1, DFT_TN), const), pl.BlockSpec((1, DFT_TN), const)],
        out_specs=tile(rows),
        out_shape=jax.ShapeDtypeStruct((rows, m), F32),
        compiler_params=_cparams("parallel"),
        name="hy_idft_rows",
    )(bre, bim, wmat, gate, zin, bias_t, sum_t)


CTX_N = 2 * CTX_LEN


def _hy_ctx_kernel(u_ref, k_ref, c_ref, s_ref, gate_ref, bias_ref, sum_ref, o_ref):
    cp = _split2(c_ref[...])
    sp = _split2(s_ref[...])
    head = lambda pair: (pair[0][:, 0:CTX_LEN], pair[1][:, 0:CTX_LEN])
    top = lambda pair: (pair[0][0:CTX_LEN, :], pair[1][0:CTX_LEN, :])
    u = u_ref[...]
    up = _split2(u)
    ur = _dot3(head(cp), up)
    ui = -_dot3(head(sp), up)
    kp = _split2(k_ref[...])
    kr = _dot3(cp, kp)
    ki = -_dot3(sp, kp)
    conv = (_dot3(top(cp), _split2(ur * kr - ui * ki)) - _dot3(top(sp), _split2(ur * ki + ui * kr)))
    o_ref[...] = gate_ref[...] * (conv / (CTX_N * sum_ref[...]) + bias_ref[...] * u)


def hy_ctx_conv(u, kfilt, order, cmat, smat, gate, bias, sums):
    const = lambda i: (0, 0)
    chan = pl.BlockSpec((CTX_LEN, HY_WIDTH), const)
    vec = pl.BlockSpec((1, HY_WIDTH), const)
    return pl.pallas_call(
        _hy_ctx_kernel,
        grid=(1,),
        in_specs=[chan, pl.BlockSpec((CTX_N, HY_WIDTH), lambda i: (0, order)),
                  pl.BlockSpec((CTX_N, CTX_N), const), pl.BlockSpec((CTX_N, CTX_N), const), chan, vec, vec],
        out_specs=chan,
        out_shape=jax.ShapeDtypeStruct((CTX_LEN, HY_WIDTH), F32),
        compiler_params=_cparams("arbitrary"),
        name="hy_ctx_conv",
    )(u, kfilt, cmat, smat, gate, bias, sums)


def _rope_tables():
    rows = SEQ // GRID_W
    r, col = jnp.meshgrid(jnp.arange(rows), jnp.arange(GRID_W), indexing='ij')
    per_axis = MLA_ROPE // 2
    inv = ROPE_THETA ** (-jnp.arange(0, per_axis, 2, dtype=F32) / per_axis)
    ang = jnp.concatenate([r.reshape(-1, 1) * inv, col.reshape(-1, 1) * inv], axis=-1)
    cos = jnp.concatenate([jnp.cos(ang), jnp.ones((CTX_LEN, per_axis), F32)], axis=0)
    sin = jnp.concatenate([jnp.sin(ang), jnp.zeros((CTX_LEN, per_axis), F32)], axis=0)
    z = jnp.zeros_like(cos)
    return jnp.concatenate([cos, z, cos, z], axis=-1), jnp.concatenate([-sin, z, sin, z], axis=-1)


def _permute_w_in(w_in):
    z, xbc, dt, cq, ckv, kr, hy = jnp.split(w_in, np.cumsum(
        (SSD_WIDTH, SSD_XBC, 2 * SSD_HEADS, MLA_Q_RANK, MLA_KV_RANK, MLA_ROPE)).tolist(), axis=-1)
    zeros = lambda n: jnp.zeros(w_in.shape[:-1] + (n,), w_in.dtype)
    return jnp.concatenate([hy, z, ckv, xbc, kr[..., :32], zeros(32), kr[..., 32:], zeros(32),
                            dt, zeros(128 - 2 * SSD_HEADS), cq], axis=-1).astype(BF16)


def _pad_w_uq(w_uq):
    w = w_uq.reshape(DEPTH, MLA_Q_RANK, MLA_HEADS, MLA_QK)
    zeros = jnp.zeros(w.shape[:-1] + (32,), w.dtype)
    w = jnp.concatenate([w[..., :MLA_NOPE], w[..., MLA_NOPE:MLA_NOPE + 32], zeros,
                         w[..., MLA_NOPE + 32:], zeros], axis=-1)
    return w.reshape(DEPTH, MLA_Q_RANK, MLA_HEADS * QK_PAD).astype(BF16)


def kernel(x, c, ctx, c_ctx, w_mod, b_mod, norm1_w, norm2_w, w_in, w_out, ssd_conv_w, ssd_conv_b,
           ssd_dt_bias, ssd_a_log, ssd_d, ssd_norm_w, mla_q_norm_w, mla_w_uq, mla_kv_norm_w, mla_w_ukv,
           hy_short_w, hy_short_b, hy_w1, hy_b1, hy_w2, hy_b2, hy_w3, hy_b3, hy_w_out, hy_freq, hy_bias,
           moe_router, moe_w1, moe_w3, moe_w2, final_norm_w):
    xj = jnp.concatenate([x[0], ctx[0]], axis=0)
    cvec = jnp.concatenate([c, c_ctx[None, :], jnp.zeros((6, D_MODEL), F32)], axis=0)
    mods = modulation(cvec, w_mod, b_mod)
    c2, s2 = _rope_tables()
    w_in_p = _permute_w_in(w_in)
    w_uq_p = _pad_w_uq(mla_w_uq)
    w_ukv_b = mla_w_ukv.astype(BF16)
    w_out_b = w_out.astype(BF16)
    router_p = jnp.pad(moe_router, ((0, 0), (0, 0), (0, 128 - N_EXPERTS)))
    lane_pad = lambda a: jnp.pad(a, ((0, 0), (0, 0), (0, 128 - SSD_HEADS)))[:, :, None, :]
    dt_bias = lane_pad(ssd_dt_bias)
    a_log = lane_pad(ssd_a_log)
    d_skip = jnp.repeat(ssd_d, SSD_HEAD_DIM, axis=-1)[:, None, :]
    tri_lower = jnp.asarray(np.tril(np.ones((SSD_CHUNK, SSD_CHUNK), np.float32)), BF16)
    tri_upper = jnp.asarray(np.triu(np.ones((SSD_CHUNK, SSD_CHUNK), np.float32)), BF16)
    lstrict = jnp.asarray(np.tril(np.ones((ROUTE_TB, ROUTE_TB), np.float32), -1), BF16)
    tv = np.zeros((128, 128), np.float32)
    tv[:, 0] = np.arange(128)
    tv[:, 1] = 1.0
    tvals = jnp.asarray(tv, BF16)

    pad_to = lambda a, r, c_: jnp.pad(a, ((0, 0), (0, r - a.shape[1]), (0, c_ - a.shape[2])))
    hf_w1, hf_w2, hf_w3 = pad_to(hy_w1, 128, 128), pad_to(hy_w2, 128, 128), pad_to(hy_w3, 128, 128)
    hf_b1, hf_b2, hf_b3, hf_fr = (pad_to(a[:, None, :], 1, 128) for a in (hy_b1, hy_b2, hy_b3, hy_freq))
    hf_wo = hy_w_out.reshape(DEPTH, HY_FFN, HY_ORDER, 2, HY_WIDTH).transpose(0, 1, 3, 2, 4)
    hf_wo = pad_to(hf_wo.reshape(DEPTH, HY_FFN, 2 * HY_COLS), 128, 2 * HY_COLS)
    band_vals = jnp.linspace(1e-4, HY_BANDS - 1, HY_BANDS, dtype=F32)
    bands = jnp.concatenate([jnp.zeros((1,), F32), band_vals, band_vals,
                             jnp.zeros((128 - HY_EMB,), F32)])[None, :]
    deltas = jnp.abs(jnp.linspace(math.log(HY_TARGET) / HY_SLOW_DECAY_PCT,
                                  math.log(HY_TARGET) / HY_FAST_DECAY_PCT, HY_WIDTH, dtype=F32))
    deltas = jnp.tile(deltas, HY_ORDER)[None, :]
    jk = np.outer(np.arange(FFT_R), np.arange(FFT_R))
    cos_r = np.cos(2 * np.pi * (jk % FFT_R) / FFT_R)
    sin_r = np.sin(2 * np.pi * (jk % FFT_R) / FFT_R)
    half = FFT_R // 2
    dft_in_full = jnp.asarray(np.concatenate([cos_r, -sin_r], axis=0), F32)
    dft_in_half = jnp.asarray(np.concatenate([cos_r[:, :half], -sin_r[:, :half]], axis=0), F32)
    dft_fwd = jnp.asarray(np.block([[cos_r, sin_r], [-sin_r, cos_r]]), F32)
    dft_inv = jnp.asarray(np.block([[cos_r, -sin_r], [sin_r, cos_r]]), F32)
    dft_out_half = jnp.asarray(np.concatenate([cos_r[:half], -sin_r[:half]], axis=1), F32)
    tw_cos = jnp.asarray(np.cos(2 * np.pi * jk / FFT_N), F32)[:, :, None]
    tw_sin = jnp.asarray(np.sin(2 * np.pi * jk / FFT_N), F32)[:, :, None]
    jk_c = np.outer(np.arange(CTX_N), np.arange(CTX_N)) % CTX_N
    cos_c = jnp.asarray(np.cos(2 * np.pi * jk_c / CTX_N), F32)
    sin_c = jnp.asarray(np.sin(2 * np.pi * jk_c / CTX_N), F32)

    for i in range(DEPTH):
        mod = mods[i]
        p = in_projection(xj, mod, norm1_w[i][None, :], w_in_p[i])
        q, k, v = mla_projection(p, c2, s2, mla_q_norm_w[i][None, :], mla_kv_norm_w[i][None, :],
                                 w_uq_p[i], w_ukv_b[i])
        y_mla = attention(q, k, v)

        xbc = dwconv(p, P_XBC // SSD_XBC, SSD_XBC, ssd_conv_w[i], ssd_conv_b[i][None, :], act=True)
        y_f = ssd_scan(xbc, p, dt_bias[i, 0], a_log[i, 0], tri_lower, reverse=False)
        y_ssd = ssd_scan(xbc, p, dt_bias[i, 1], a_log[i, 1], tri_upper, reverse=True, yf=y_f,
                         dskip=d_skip[i], nw=ssd_norm_w[i][None, :])

        vxx = [dwconv(p, j, HY_WIDTH, hy_short_w[i][:, j * HY_WIDTH:(j + 1) * HY_WIDTH],
                      hy_short_b[i][None, j * HY_WIDTH:(j + 1) * HY_WIDTH], act=False) for j in range(3)]
        filt = (hf_w1[i], hf_b1[i], hf_w2[i], hf_b2[i], hf_w3[i], hf_b3[i], hf_wo[i], hf_fr[i], bands, deltas)
        k_l, sum_l = hy_filter(SEQ, *filt)
        k_c, sum_c = hy_filter(CTX_LEN, *filt)
        are, aim = dft_rows(k_l.reshape(FFT_R, FFT_R * HY_COLS), dft_in_full)
        kf = fft_mid(are.reshape(FFT_R, FFT_R, HY_COLS), aim.reshape(FFT_R, FFT_R, HY_COLS),
                     tw_cos, tw_sin, dft_fwd, dft_inv)
        z_l = vxx[0][:SEQ].reshape(half, FFT_R * HY_WIDTH)
        z_c = vxx[0][SEQ:]
        for o in range(HY_ORDER):
            ch = slice(o * HY_WIDTH, (o + 1) * HY_WIDTH)
            are, aim = dft_rows(z_l, dft_in_half)
            bre, bim = fft_mid(are.reshape(FFT_R, FFT_R, HY_WIDTH), aim.reshape(FFT_R, FFT_R, HY_WIDTH),
                               tw_cos, tw_sin, dft_fwd, dft_inv, kf=kf, order=o)
            z_l = idft_rows_gated(bre.reshape(FFT_R, FFT_R * HY_WIDTH), bim.reshape(FFT_R, FFT_R * HY_WIDTH),
                                  dft_out_half,
                                  vxx[1 + o][:SEQ].reshape(half, FFT_R * HY_WIDTH), z_l,
                                  jnp.tile(hy_bias[i, o][None, :], (1, DFT_TN // HY_WIDTH)),
                                  jnp.tile(sum_l[0:1, ch], (1, DFT_TN // HY_WIDTH)))
            z_c = hy_ctx_conv(z_c, k_c, o, cos_c, sin_c, vxx[1 + o][SEQ:], hy_bias[i, o][None, :], sum_c[0:1, ch])
        y_hy = jnp.concatenate([z_l.reshape(SEQ, HY_WIDTH), z_c], axis=0)

        x1, h2, aff = out_projection(y_ssd, y_mla, y_hy, xj, mod, norm2_w[i][None, :], w_out_b[i], router_p[i])
        pos, ws = route(aff, lstrict)
        found = compact(ws, pos, tvals)
        idx = (found[:, :SLOTS, 0] + found[:, :SLOTS, 1]).astype(jnp.int32)
        y_moe = expert_ffn(idx, h2, moe_w1, moe_w3, moe_w2, i)
        xj = combine(ws, y_moe, pos, aff, x1, mod)
    return final_norm(xj, final_norm_w[None, :])[None]
```

```python
import functools
import math

import jax
import jax.numpy as jnp
import numpy as np
from jax import lax
from jax.experimental import pallas as pl
from jax.experimental.pallas import tpu as pltpu

D_MODEL = 2048
SEQ = 8192
DEPTH = 4
GRID_W = 64
CTX_LEN = 256
T_ALL = SEQ + CTX_LEN
EPS = 1e-6
N_MOD = 6

SSD_HEAD_DIM = 64
SSD_WIDTH = 768
SSD_HEADS = 12
SSD_GROUPS = 2
SSD_STATE = 128
SSD_CONV = 5
SSD_CHUNK = 128
SSD_XBC = SSD_WIDTH + 2 * SSD_GROUPS * SSD_STATE
MLA_NOPE = 128
MLA_ROPE = 64
MLA_V = 128
MLA_QK = MLA_NOPE + MLA_ROPE
MLA_WIDTH = 768
MLA_HEADS = 6
MLA_Q_RANK = 512
MLA_KV_RANK = 256
ROPE_THETA = 10000.0
HY_WIDTH = 512
HY_ORDER = 2
HY_SHORT = 3
HY_BANDS = 16
HY_EMB = 1 + 2 * HY_BANDS
HY_FFN = 64
HY_TARGET = 1e-2
HY_FAST_DECAY_PCT = 0.3
HY_SLOW_DECAY_PCT = 1.5
N_EXPERTS = 16
EXPERT_FF = 1024
CAPACITY_FACTOR = 2

P_HY = 0
P_Z = 1536
P_CKV = 2304
P_XBC = 2560
P_KR = 3840
P_DT = 3968
P_CQ = 4096
P_COLS = 4608
QK_PAD = 256
V_PAD = 256
ATT_SCALE = 1.0 / math.sqrt(MLA_QK)
CAP_L = CAPACITY_FACTOR * SEQ // N_EXPERTS
CAP_C = CAPACITY_FACTOR * CTX_LEN // N_EXPERTS
SLOTS = 1152
N_TB = T_ALL // 128

F32 = jnp.float32
BF16 = jnp.bfloat16
VMEM_LIMIT = 56 * 1024 * 1024


def _cparams(*sem):
    return pltpu.CompilerParams(dimension_semantics=sem, vmem_limit_bytes=VMEM_LIMIT)


def _split3(a):
    hi = a.astype(BF16)
    r = a - hi.astype(F32)
    mid = r.astype(BF16)
    lo = (r - mid.astype(F32)).astype(BF16)
    return hi, mid, lo


def _dot(a, b):
    return jnp.dot(a, b, preferred_element_type=F32)


def _dot_hp(a, b):
    a1, a2, a3 = _split3(a)
    b1, b2, b3 = _split3(b)
    return (_dot(a1, b1) + (_dot(a1, b2) + _dot(a2, b1))
            + (_dot(a2, b2) + _dot(a1, b3) + _dot(a3, b1)))


def _silu(v):
    return v * (1.0 / (1.0 + jnp.exp(-v)))


def _mod_kernel(c_ref, w_ref, b_ref, o_ref):
    s = _silu(c_ref[...])
    o_ref[0] = _dot_hp(s, w_ref[0]) + b_ref[0]


def modulation(cvec, w_mod, b_mod):
    tn = 1024
    return pl.pallas_call(
        _mod_kernel,
        grid=(DEPTH, N_MOD * D_MODEL // tn),
        in_specs=[pl.BlockSpec((8, D_MODEL), lambda l, j: (0, 0)),
                  pl.BlockSpec((1, D_MODEL, tn), lambda l, j: (l, 0, j)),
                  pl.BlockSpec((1, 1, tn), lambda l, j: (l, 0, j))],
        out_specs=pl.BlockSpec((1, 8, tn), lambda l, j: (l, 0, j)),
        out_shape=jax.ShapeDtypeStruct((DEPTH, 8, N_MOD * D_MODEL), F32),
        compiler_params=_cparams("parallel", "parallel"),
        name="modulation",
    )(cvec, w_mod, b_mod.reshape(DEPTH, 1, N_MOD * D_MODEL))


def _norm_mod(x, nw, mod_ref, row0, k_shift, k_scale):
    n = x.shape[0]
    xn = x * lax.rsqrt(jnp.mean(x * x, axis=-1, keepdims=True) + EPS) * nw
    is_ctx = (row0 + lax.broadcasted_iota(jnp.int32, (n, 1), 0)) >= SEQ
    sl_shift = slice(k_shift * D_MODEL, (k_shift + 1) * D_MODEL)
    sl_scale = slice(k_scale * D_MODEL, (k_scale + 1) * D_MODEL)
    shift = jnp.where(is_ctx, mod_ref[1:2, sl_shift], mod_ref[0:1, sl_shift])
    scale = jnp.where(is_ctx, mod_ref[1:2, sl_scale], mod_ref[0:1, sl_scale])
    return xn * (1.0 + scale) + shift


IN_TM = 768
IN_TN = 768


def _inproj_kernel(x_ref, mod_ref, nw_ref, w_ref, o_ref, h_ref):
    @pl.when(pl.program_id(1) == 0)
    def _():
        h = _norm_mod(x_ref[...], nw_ref[...], mod_ref, pl.program_id(0) * IN_TM, 0, 1)
        h_ref[...] = h.astype(BF16)

    o_ref[...] = _dot(h_ref[...], w_ref[...])


def in_projection(x, mod, nw, w):
    return pl.pallas_call(
        _inproj_kernel,
        grid=(T_ALL // IN_TM, P_COLS // IN_TN),
        in_specs=[pl.BlockSpec((IN_TM, D_MODEL), lambda i, j: (i, 0)),
                  pl.BlockSpec((8, N_MOD * D_MODEL), lambda i, j: (0, 0)),
                  pl.BlockSpec((1, D_MODEL), lambda i, j: (0, 0)),
                  pl.BlockSpec((D_MODEL, IN_TN), lambda i, j: (0, j))],
        out_specs=pl.BlockSpec((IN_TM, IN_TN), lambda i, j: (i, j)),
        out_shape=jax.ShapeDtypeStruct((T_ALL, P_COLS), F32),
        scratch_shapes=[pltpu.VMEM((IN_TM, D_MODEL), BF16)],
        compiler_params=_cparams("parallel", "arbitrary"),
        name="in_projection",
    )(x, mod, nw, w)


MLA_TM = 768


def _rope(v, c2, s2):
    return v * c2 + pltpu.roll(v, 64, 1) * s2


def _mla_proj_kernel(cq_ref, ckv_ref, kr_ref, c2_ref, s2_ref, qnw_ref, kvnw_ref, wq_ref, wkv_ref,
                     q_ref, k_ref, v_ref):
    c2 = c2_ref[...]
    s2 = s2_ref[...]
    cq = cq_ref[...]
    cqn = cq * lax.rsqrt(jnp.mean(cq * cq, axis=-1, keepdims=True) + EPS) * qnw_ref[...]
    q = _dot(cqn.astype(BF16), wq_ref[...]) * ATT_SCALE
    ckv = ckv_ref[...]
    ckvn = ckv * lax.rsqrt(jnp.mean(ckv * ckv, axis=-1, keepdims=True) + EPS) * kvnw_ref[...]
    kv = _dot(ckvn.astype(BF16), wkv_ref[...])
    kr = _rope(kr_ref[...], c2, s2).astype(BF16)
    for h in range(MLA_HEADS):
        b = h * QK_PAD
        q_ref[h, :, 0:MLA_NOPE] = q[:, b:b + MLA_NOPE].astype(BF16)
        q_ref[h, :, MLA_NOPE:QK_PAD] = _rope(q[:, b + MLA_NOPE:b + QK_PAD], c2, s2).astype(BF16)
        k_ref[h, :, 0:MLA_NOPE] = kv[:, b:b + MLA_NOPE].astype(BF16)
        k_ref[h, :, MLA_NOPE:QK_PAD] = kr
        v_ref[h, :, 0:MLA_V] = kv[:, b + MLA_NOPE:b + QK_PAD].astype(BF16)
        v_ref[h, :, MLA_V:V_PAD] = jnp.ones((MLA_TM, V_PAD - MLA_V), BF16)


def mla_projection(p, c2, s2, qnw, kvnw, wq, wkv):
    rows = lambda i: (i, 0)
    const = lambda i: (0, 0)
    return pl.pallas_call(
        _mla_proj_kernel,
        grid=(T_ALL // MLA_TM,),
        in_specs=[pl.BlockSpec((MLA_TM, MLA_Q_RANK), lambda i: (i, P_CQ // MLA_Q_RANK)),
                  pl.BlockSpec((MLA_TM, MLA_KV_RANK), lambda i: (i, P_CKV // MLA_KV_RANK)),
                  pl.BlockSpec((MLA_TM, 128), lambda i: (i, P_KR // 128)),
                  pl.BlockSpec((MLA_TM, 128), rows),
                  pl.BlockSpec((MLA_TM, 128), rows),
                  pl.BlockSpec((1, MLA_Q_RANK), const),
                  pl.BlockSpec((1, MLA_KV_RANK), const),
                  pl.BlockSpec((MLA_Q_RANK, MLA_HEADS * QK_PAD), const),
                  pl.BlockSpec((MLA_KV_RANK, MLA_HEADS * QK_PAD), const)],
        out_specs=[pl.BlockSpec((MLA_HEADS, MLA_TM, QK_PAD), lambda i: (0, i, 0)),
                   pl.BlockSpec((MLA_HEADS, MLA_TM, QK_PAD), lambda i: (0, i, 0)),
                   pl.BlockSpec((MLA_HEADS, MLA_TM, V_PAD), lambda i: (0, i, 0))],
        out_shape=[jax.ShapeDtypeStruct((MLA_HEADS, T_ALL, QK_PAD), BF16),
                   jax.ShapeDtypeStruct((MLA_HEADS, T_ALL, QK_PAD), BF16),
                   jax.ShapeDtypeStruct((MLA_HEADS, T_ALL, V_PAD), BF16)],
        compiler_params=_cparams("parallel"),
        name="mla_projection",
    )(p, p, p, c2, s2, qnw, kvnw, wq, wkv)


ATT_TQ = 1024
ATT_TK = 512


def _attn_scores(q, k):
    return lax.dot_general(q, k, (((1,), (1,)), ((), ())), preferred_element_type=F32)


def _attn_update(s, v, m, acc):
    m_new = jnp.maximum(m, jnp.max(s, axis=-1, keepdims=True))
    alpha = jnp.exp(m - m_new)
    p = jnp.exp(s - m_new).astype(BF16)
    return m_new, alpha * acc + _dot(p, v)


def _attn_finish(acc, o_ref):
    o_ref[...] = acc[:, 0:MLA_V] / acc[:, MLA_V:MLA_V + 1]


def _attention_latent_kernel(q_ref, k_ref, v_ref, o_ref, s_ref):
    q = q_ref[0]
    n = SEQ // ATT_TK

    def rows(c):
        return pl.ds(pl.multiple_of(c * ATT_TK, ATT_TK), ATT_TK)

    def body(c2, carry):
        m, acc = carry
        c = 2 * c2
        s_ref[1] = _attn_scores(q, k_ref[0, rows(c + 1), :])
        m, acc = _attn_update(s_ref[0], v_ref[0, rows(c), :], m, acc)
        s_ref[0] = _attn_scores(q, k_ref[0, rows(c + 2), :])
        return _attn_update(s_ref[1], v_ref[0, rows(c + 1), :], m, acc)

    s_ref[0] = _attn_scores(q, k_ref[0, 0:ATT_TK, :])
    init = (jnp.full((ATT_TQ, 1), -1e30, F32), jnp.zeros((ATT_TQ, V_PAD), F32))
    m, acc = lax.fori_loop(0, n // 2 - 1, body, init)
    s_ref[1] = _attn_scores(q, k_ref[0, (n - 1) * ATT_TK:n * ATT_TK, :])
    s_ctx = _attn_scores(q, k_ref[0, SEQ:T_ALL, :])
    m, acc = _attn_update(s_ref[0], v_ref[0, (n - 2) * ATT_TK:(n - 1) * ATT_TK, :], m, acc)
    m, acc = _attn_update(s_ref[1], v_ref[0, (n - 1) * ATT_TK:n * ATT_TK, :], m, acc)
    m, acc = _attn_update(s_ctx, v_ref[0, SEQ:T_ALL, :], m, acc)
    _attn_finish(acc, o_ref)


def _attention_context_kernel(q_ref, k_ref, v_ref, o_ref):
    init = (jnp.full((CTX_LEN, 1), -1e30, F32), jnp.zeros((CTX_LEN, V_PAD), F32))
    m, acc = _attn_update(_attn_scores(q_ref[0], k_ref[0]), v_ref[0], *init)
    _attn_finish(acc, o_ref)


def attention(q, k, v):
    y_l = pl.pallas_call(
        _attention_latent_kernel,
        grid=(MLA_HEADS, SEQ // ATT_TQ),
        in_specs=[pl.BlockSpec((1, ATT_TQ, QK_PAD), lambda h, i: (h, i, 0)),
                  pl.BlockSpec((1, T_ALL, QK_PAD), lambda h, i: (h, 0, 0)),
                  pl.BlockSpec((1, T_ALL, V_PAD), lambda h, i: (h, 0, 0))],
        out_specs=pl.BlockSpec((ATT_TQ, MLA_V), lambda h, i: (i, h)),
        out_shape=jax.ShapeDtypeStruct((SEQ, MLA_WIDTH), F32),
        scratch_shapes=[pltpu.VMEM((2, ATT_TQ, ATT_TK), F32)],
        compiler_params=_cparams("parallel", "parallel"),
        name="attention_latent",
    )(q, k, v)
    cblk = SEQ // CTX_LEN
    y_c = pl.pallas_call(
        _attention_context_kernel,
        grid=(MLA_HEADS,),
        in_specs=[pl.BlockSpec((1, CTX_LEN, QK_PAD), lambda h: (h, cblk, 0)),
                  pl.BlockSpec((1, CTX_LEN, QK_PAD), lambda h: (h, cblk, 0)),
                  pl.BlockSpec((1, CTX_LEN, V_PAD), lambda h: (h, cblk, 0))],
        out_specs=pl.BlockSpec((CTX_LEN, MLA_V), lambda h: (0, h)),
        out_shape=jax.ShapeDtypeStruct((CTX_LEN, MLA_WIDTH), F32),
        compiler_params=_cparams("parallel"),
        name="attention_context",
    )(q, k, v)
    return jnp.concatenate([y_l, y_c], axis=0)


OUT_TM = 384


def _outproj_kernel(ys_ref, ym_ref, yh_ref, x_ref, mod_ref, nw_ref, w_ref, r_ref, x1_ref, h2_ref, aff_ref):
    y = _dot(ys_ref[...].astype(BF16), w_ref[0:SSD_WIDTH, :])
    y += _dot(ym_ref[...].astype(BF16), w_ref[SSD_WIDTH:SSD_WIDTH + MLA_WIDTH, :])
    y += _dot(yh_ref[...].astype(BF16), w_ref[SSD_WIDTH + MLA_WIDTH:D_MODEL, :])
    row0 = pl.program_id(0) * OUT_TM
    is_ctx = (row0 + lax.broadcasted_iota(jnp.int32, (OUT_TM, 1), 0)) >= SEQ
    gate = jnp.where(is_ctx, mod_ref[1:2, 2 * D_MODEL:3 * D_MODEL], mod_ref[0:1, 2 * D_MODEL:3 * D_MODEL])
    x1 = x_ref[...] + gate * y
    x1_ref[...] = x1
    h2 = _norm_mod(x1, nw_ref[...], mod_ref, row0, 3, 4)
    h2_ref[...] = h2
    logits = _dot3(_split2(h2), _split2(r_ref[...]))
    lane = lax.broadcasted_iota(jnp.int32, logits.shape, 1)
    logits = jnp.where(lane < N_EXPERTS, logits, -1e30)
    e = jnp.exp(logits - jnp.max(logits, axis=-1, keepdims=True))
    aff_ref[...] = e / jnp.sum(e, axis=-1, keepdims=True)


def out_projection(y_ssd, y_mla, y_hy, x, mod, nw, w, router):
    rows = lambda i: (i, 0)
    const = lambda i: (0, 0)
    return pl.pallas_call(
        _outproj_kernel,
        grid=(T_ALL // OUT_TM,),
        in_specs=[pl.BlockSpec((OUT_TM, SSD_WIDTH), rows),
                  pl.BlockSpec((OUT_TM, MLA_WIDTH), rows),
                  pl.BlockSpec((OUT_TM, HY_WIDTH), rows),
                  pl.BlockSpec((OUT_TM, D_MODEL), rows),
                  pl.BlockSpec((8, N_MOD * D_MODEL), const),
                  pl.BlockSpec((1, D_MODEL), const),
                  pl.BlockSpec((D_MODEL, D_MODEL), const),
                  pl.BlockSpec((D_MODEL, 128), const)],
        out_specs=[pl.BlockSpec((OUT_TM, D_MODEL), rows),
                   pl.BlockSpec((OUT_TM, D_MODEL), rows),
                   pl.BlockSpec((OUT_TM, 128), rows)],
        out_shape=[jax.ShapeDtypeStruct((T_ALL, D_MODEL), F32),
                   jax.ShapeDtypeStruct((T_ALL, D_MODEL), F32),
                   jax.ShapeDtypeStruct((T_ALL, 128), F32)],
        compiler_params=_cparams("parallel"),
        name="out_projection",
    )(y_ssd, y_mla, y_hy, x, mod, nw, w, router)


def _final_norm_kernel(x_ref, w_ref, o_ref):
    x = x_ref[...]
    o_ref[...] = x * lax.rsqrt(jnp.mean(x * x, axis=-1, keepdims=True) + EPS) * w_ref[...]


def final_norm(x, w):
    tm = 512
    return pl.pallas_call(
        _final_norm_kernel,
        grid=(SEQ // tm,),
        in_specs=[pl.BlockSpec((tm, D_MODEL), lambda i: (i, 0)), pl.BlockSpec((1, D_MODEL), lambda i: (0, 0))],
        out_specs=pl.BlockSpec((tm, D_MODEL), lambda i: (i, 0)),
        out_shape=jax.ShapeDtypeStruct((SEQ, D_MODEL), F32),
        compiler_params=_cparams("parallel"),
        name="final_norm",
    )(x, w)


CONV_TM = 256
CONV_HALO = 8


def _conv_kernel(main_ref, prev_ref, next_ref, w_ref, b_ref, o_ref, ext_ref, *, taps, act):
    i = pl.program_id(0)
    n_lat = SEQ // CONV_TM
    has_prev = jnp.logical_and(i != 0, i != n_lat)
    has_next = jnp.logical_and(i != n_lat - 1, i != n_lat)
    ext_ref[0:CONV_HALO, :] = jnp.where(has_prev, prev_ref[...], 0.0)
    ext_ref[CONV_HALO:CONV_HALO + CONV_TM, :] = main_ref[...]
    ext_ref[CONV_HALO + CONV_TM:2 * CONV_HALO + CONV_TM, :] = jnp.where(has_next, next_ref[...], 0.0)
    first = CONV_HALO - taps // 2
    acc = w_ref[0:1, :] * ext_ref[first:first + CONV_TM, :] + b_ref[...]
    for k in range(1, taps):
        acc = acc + w_ref[k:k + 1, :] * ext_ref[first + k:first + k + CONV_TM, :]
    o_ref[...] = _silu(acc) if act else acc


def dwconv(p, col_block, width, w, b, act):
    taps = w.shape[0]
    r = CONV_TM // CONV_HALO
    last = T_ALL // CONV_HALO - 1
    return pl.pallas_call(
        functools.partial(_conv_kernel, taps=taps, act=act),
        grid=(T_ALL // CONV_TM,),
        in_specs=[pl.BlockSpec((CONV_TM, width), lambda i: (i, col_block)),
                  pl.BlockSpec((CONV_HALO, width), lambda i: (jnp.maximum(i * r - 1, 0), col_block)),
                  pl.BlockSpec((CONV_HALO, width), lambda i: (jnp.minimum((i + 1) * r, last), col_block)),
                  pl.BlockSpec((taps, width), lambda i: (0, 0)),
                  pl.BlockSpec((1, width), lambda i: (0, 0))],
        out_specs=pl.BlockSpec((CONV_TM, width), lambda i: (i, 0)),
        out_shape=jax.ShapeDtypeStruct((T_ALL, width), F32),
        scratch_shapes=[pltpu.VMEM((CONV_TM + 2 * CONV_HALO, width), F32)],
        compiler_params=_cparams("parallel"),
        name="dwconv",
    )(p, p, p, w, b)


N_CHUNK = T_ALL // SSD_CHUNK
N_CTX_CHUNK = CTX_LEN // SSD_CHUNK
SSD_PAIRS = SSD_HEADS // 2


def _ssd_kernel(*refs, reverse):
    if reverse:
        (xbc_ref, dt_ref, bias_ref, alog_ref, tri_ref, yf_ref, z_ref, dskip_ref, nw_ref, y_ref, s_ref) = refs
    else:
        (xbc_ref, dt_ref, bias_ref, alog_ref, tri_ref, y_ref, s_ref) = refs
    q = SSD_CHUNK

    @pl.when(pl.program_id(0) == 0)
    def _():
        s_ref[...] = jnp.zeros_like(s_ref)

    dtraw = dt_ref[...]
    if reverse:
        dtraw = pltpu.roll(dtraw, 128 - SSD_HEADS, 1)
    v = dtraw + bias_ref[...]
    dt = jnp.maximum(v, 0.0) + jnp.log(1.0 + jnp.exp(-jnp.abs(v)))
    da = dt * (-jnp.exp(alog_ref[...]))
    d1, d2, d3 = _split3(da)
    tri = tri_ref[...]
    acum = _dot(tri, d1) + _dot(tri, d2) + _dot(tri, d3)
    acum_t = acum.T
    dt_t = dt.T
    last = 0 if reverse else q - 1
    ii = lax.broadcasted_iota(jnp.int32, (q, q), 0)
    jj = lax.broadcasted_iota(jnp.int32, (q, q), 1)
    mask = (jj >= ii) if reverse else (jj <= ii)
    left = lax.broadcasted_iota(jnp.int32, (q, 128), 1) < SSD_HEAD_DIM
    left_row = lax.broadcasted_iota(jnp.int32, (1, 128), 1) < SSD_HEAD_DIM

    for g in range(SSD_GROUPS):
        bg = xbc_ref[:, SSD_WIDTH + g * SSD_STATE:SSD_WIDTH + (g + 1) * SSD_STATE]
        cg = xbc_ref[:, SSD_WIDTH + (SSD_GROUPS + g) * SSD_STATE:SSD_WIDTH + (SSD_GROUPS + g + 1) * SSD_STATE]
        bgb = bg.astype(BF16)
        cgb = cg.astype(BF16)
        cb = lax.dot_general(cgb, bgb, (((1,), (1,)), ((), ())), preferred_element_type=F32)
        bg_t = bg.T.astype(BF16)
        for pr in range(g * SSD_PAIRS // SSD_GROUPS, (g + 1) * SSD_PAIRS // SSD_GROUPS):
            sl = slice(pr * 128, (pr + 1) * 128)
            xp = xbc_ref[:, sl]
            xpb = xp.astype(BF16)
            heads = (2 * pr, 2 * pr + 1)

            def scores(h):
                seg = acum[:, h:h + 1] - acum_t[h:h + 1, :]
                decay = jnp.exp(jnp.where(mask, seg, -1e30))
                return (cb * decay * dt_t[h:h + 1, :]).astype(BF16)

            y_diag = jnp.where(left, _dot(scores(heads[0]), xpb), _dot(scores(heads[1]), xpb))
            col = [acum[:, h:h + 1] for h in heads]
            tot = [acum_t[h:h + 1, last:last + 1] for h in heads]
            sp = s_ref[:, sl]
            y_off = _dot(cgb, sp.astype(BF16)) * jnp.where(left, jnp.exp(col[0]), jnp.exp(col[1]))
            y_ref[:, sl] = y_diag + y_off
            wgt = jnp.where(left, jnp.exp(tot[0] - col[0]) * dt[:, heads[0]:heads[0] + 1],
                            jnp.exp(tot[1] - col[1]) * dt[:, heads[1]:heads[1] + 1])
            dec = jnp.where(left_row, jnp.exp(tot[0]), jnp.exp(tot[1]))
            s_ref[:, sl] = sp * dec + _dot(bg_t, (xp * wgt).astype(BF16))

    if reverse:
        y = (y_ref[...] + yf_ref[...] + dskip_ref[...] * xbc_ref[:, 0:SSD_WIDTH]) * _silu(z_ref[...])
        y_ref[...] = y * lax.rsqrt(jnp.mean(y * y, axis=-1, keepdims=True) + EPS) * nw_ref[...]


def ssd_scan(xbc, p, bias, alog, tri, reverse, yf=None, dskip=None, nw=None):
    if reverse:
        order = lambda s: N_CHUNK - 1 - s
    else:
        order = lambda s: jnp.where(s < N_CTX_CHUNK, N_CHUNK - N_CTX_CHUNK + s, s - N_CTX_CHUNK)
    const = lambda s: (0, 0)
    in_specs = [pl.BlockSpec((SSD_CHUNK, SSD_XBC), lambda s: (order(s), 0)),
                pl.BlockSpec((SSD_CHUNK, 128), lambda s: (order(s), P_DT // 128)),
                pl.BlockSpec((1, 128), const),
                pl.BlockSpec((1, 128), const),
                pl.BlockSpec((SSD_CHUNK, SSD_CHUNK), const)]
    args = [xbc, p, bias, alog, tri]
    if reverse:
        in_specs += [pl.BlockSpec((SSD_CHUNK, SSD_WIDTH), lambda s: (order(s), 0)),
                     pl.BlockSpec((SSD_CHUNK, SSD_WIDTH), lambda s: (order(s), P_Z // SSD_WIDTH)),
                     pl.BlockSpec((1, SSD_WIDTH), const),
                     pl.BlockSpec((1, SSD_WIDTH), const)]
        args += [yf, p, dskip, nw]
    return pl.pallas_call(
        functools.partial(_ssd_kernel, reverse=reverse),
        grid=(N_CHUNK,),
        in_specs=in_specs,
        out_specs=pl.BlockSpec((SSD_CHUNK, SSD_WIDTH), lambda s: (order(s), 0)),
        out_shape=jax.ShapeDtypeStruct((T_ALL, SSD_WIDTH), F32),
        scratch_shapes=[pltpu.VMEM((SSD_STATE, SSD_WIDTH), F32)],
        compiler_params=_cparams("arbitrary"),
        name="ssd_scan_bwd" if reverse else "ssd_scan_fwd",
    )(*args)


ROUTE_TB = 256
WS_ROWS = 72


def _route_kernel(aff_ref, lstrict_ref, pos_ref, ws_ref):
    lstrict = lstrict_ref[...]
    ws_ref[...] = jnp.zeros_like(ws_ref)

    def stream(row0, nrows, k, slot_base, ws_row0):
        nblk = nrows // ROUTE_TB

        def bits(b):
            r0 = pl.multiple_of(row0 + b * ROUTE_TB, ROUTE_TB)
            return r0, pltpu.bitcast(aff_ref[pl.ds(r0, ROUTE_TB), :], jnp.int32)

        def count(pred):
            def body(b, acc):
                return acc + jnp.sum(jnp.where(pred(bits(b)[1]), 1.0, 0.0), axis=0, keepdims=True)
            return lax.fori_loop(0, nblk, body, jnp.zeros((1, 128), F32))

        def search(it, thr):
            cand = thr | jnp.left_shift(jnp.int32(1), 30 - it)
            return jnp.where(count(lambda v: v >= cand) >= k, cand, thr)

        thr = lax.fori_loop(0, 31, search, jnp.zeros((1, 128), jnp.int32))
        need = k - count(lambda v: v > thr)

        def place(b, carry):
            ties_before, picked_before = carry
            r0, v = bits(b)
            gt = v > thr
            eq = v == thr
            eqf = jnp.where(eq, 1.0, 0.0)
            tie_rank = _dot(lstrict, eqf.astype(BF16)) + ties_before
            sel = jnp.logical_or(gt, jnp.logical_and(eq, tie_rank < need))
            self_ = jnp.where(sel, 1.0, 0.0)
            rank = _dot(lstrict, self_.astype(BF16)) + picked_before
            pos_ref[pl.ds(r0, ROUTE_TB), :] = jnp.where(sel, rank + slot_base, -1.0).astype(jnp.int32)
            w = ws_row0 + 2 * b
            ws_ref[pl.ds(w, 1), :] = (picked_before + slot_base).astype(jnp.int32)
            ws_ref[pl.ds(w + 1, 1), :] = (rank[128:129, :] + slot_base).astype(jnp.int32)
            return (ties_before + jnp.sum(eqf, axis=0, keepdims=True),
                    picked_before + jnp.sum(self_, axis=0, keepdims=True))

        zero = jnp.zeros((1, 128), F32)
        lax.fori_loop(0, nblk, place, (zero, zero))

    stream(0, SEQ, CAP_L, 0.0, 0)
    stream(SEQ, CTX_LEN, CAP_C, float(CAP_L), SEQ // 128)


def route(aff, lstrict):
    return pl.pallas_call(
        _route_kernel,
        out_shape=[jax.ShapeDtypeStruct((T_ALL, 128), jnp.int32),
                   jax.ShapeDtypeStruct((WS_ROWS, 128), jnp.int32)],
        compiler_params=pltpu.CompilerParams(vmem_limit_bytes=VMEM_LIMIT),
        name="route",
    )(aff, lstrict)


ACC_ROWS = 1280


def _compact_kernel(ws_ref, pos_ref, tv_ref, acc_ref):
    acc_ref[...] = jnp.zeros_like(acc_ref)
    tv = tv_ref[...]
    lane = lax.broadcasted_iota(jnp.int32, (1, 128), 1)
    srow = lax.broadcasted_iota(jnp.int32, (256, 1), 0)

    def body(tb, carry):
        r0 = pl.multiple_of(tb * 128, 128)
        slot_of_token = pos_ref[pl.ds(r0, 128), :].astype(F32).T
        scale = jnp.where(lane == 0, 1.0, jnp.where(lane == 1, jnp.asarray(tb * 128, F32), 0.0))
        for e in range(N_EXPERTS):
            base = pl.multiple_of(jnp.minimum(ws_ref[tb, e] & -128, ACC_ROWS - 256), 128)
            onehot = jnp.where(slot_of_token[e:e + 1, :] == (srow + base).astype(F32), 1.0, 0.0)
            found = _dot(onehot.astype(BF16), tv)
            acc_ref[e, pl.ds(base, 256), :] += found * scale
        return carry

    lax.fori_loop(0, N_TB, body, 0)


def compact(ws, pos, tv):
    return pl.pallas_call(
        _compact_kernel,
        grid_spec=pltpu.PrefetchScalarGridSpec(
            num_scalar_prefetch=1,
            grid=(1,),
            in_specs=[pl.BlockSpec((T_ALL, 128), lambda i, ws: (0, 0)),
                      pl.BlockSpec((128, 128), lambda i, ws: (0, 0))],
            out_specs=pl.BlockSpec((N_EXPERTS, ACC_ROWS, 128), lambda i, ws: (0, 0, 0))),
        out_shape=jax.ShapeDtypeStruct((N_EXPERTS, ACC_ROWS, 128), F32),
        compiler_params=_cparams("arbitrary"),
        name="compact",
    )(ws, pos, tv)


FFN_TF = 256
GATHER_ROWS = 128
N_GATHER = SLOTS // GATHER_ROWS
GATHER_PLAN = ((0, 1, 2), (3, 4), (5, 6), (7, 8))
GATHER_SLOTS = max(len(c) for c in GATHER_PLAN)
assert len(GATHER_PLAN) == EXPERT_FF // FFN_TF and sum(len(c) for c in GATHER_PLAN) == N_GATHER


def _ffn_kernel(idx_ref, h_ref, w1_ref, w3_ref, w2_ref, y_ref, x_ref, acc_ref, gbuf_ref, sem_ref):
    e = pl.program_id(0)
    f = pl.program_id(1)
    par = e % 2

    def row_copy(slot, r, token):
        return pltpu.make_async_copy(h_ref.at[pl.ds(token, 1), :], gbuf_ref.at[slot, pl.ds(r, 1), :],
                                     sem_ref.at[slot])

    def issue(expert, chunks):
        for slot, chunk in enumerate(chunks):
            def body(r, carry):
                row_copy(slot, r, idx_ref[expert, chunk * GATHER_ROWS + r]).start()
                return carry
            lax.fori_loop(0, GATHER_ROWS, body, 0, unroll=8)

    def drain(buf, chunks):
        for slot, chunk in enumerate(chunks):
            def body(r, carry):
                row_copy(slot, r, 0).wait()
                return carry
            lax.fori_loop(0, GATHER_ROWS, body, 0, unroll=8)
            x_ref[buf, chunk * GATHER_ROWS:(chunk + 1) * GATHER_ROWS, :] = gbuf_ref[slot].astype(BF16)

    @pl.when(jnp.logical_and(e == 0, f == 0))
    def _():
        for chunks in GATHER_PLAN:
            issue(0, chunks)
            drain(0, chunks)

    prefetch = e + 1 < N_EXPERTS
    for step, chunks in enumerate(GATHER_PLAN):
        @pl.when(jnp.logical_and(prefetch, f == step))
        def _():
            issue(e + 1, chunks)

    x = x_ref[par]
    a = _dot(x, w1_ref[...].astype(BF16))
    b = _dot(x, w3_ref[...].astype(BF16))
    hid = (_silu(a) * b).astype(BF16)
    part = _dot(hid, w2_ref[...].astype(BF16))

    @pl.when(f == 0)
    def _():
        acc_ref[...] = part

    @pl.when(f > 0)
    def _():
        acc_ref[...] += part

    @pl.when(f == EXPERT_FF // FFN_TF - 1)
    def _():
        y_ref[...] = acc_ref[...].astype(BF16)

    for step, chunks in enumerate(GATHER_PLAN):
        @pl.when(jnp.logical_and(prefetch, f == step))
        def _():
            drain(1 - par, chunks)


def expert_ffn(idx, h2, w1, w3, w2, layer):
    return pl.pallas_call(
        _ffn_kernel,
        grid_spec=pltpu.PrefetchScalarGridSpec(
            num_scalar_prefetch=1,
            grid=(N_EXPERTS, EXPERT_FF // FFN_TF),
            in_specs=[pl.BlockSpec(memory_space=pl.ANY),
                      pl.BlockSpec((None, None, D_MODEL, FFN_TF), lambda e, f, idx: (layer, e, 0, f)),
                      pl.BlockSpec((None, None, D_MODEL, FFN_TF), lambda e, f, idx: (layer, e, 0, f)),
                      pl.BlockSpec((None, None, FFN_TF, D_MODEL), lambda e, f, idx: (layer, e, f, 0))],
            out_specs=pl.BlockSpec((None, SLOTS, D_MODEL), lambda e, f, idx: (e, 0, 0)),
            scratch_shapes=[pltpu.VMEM((2, SLOTS, D_MODEL), BF16),
                            pltpu.VMEM((SLOTS, D_MODEL), F32),
                            pltpu.VMEM((GATHER_SLOTS, GATHER_ROWS, D_MODEL), F32),
                            pltpu.SemaphoreType.DMA((GATHER_SLOTS,))]),
        out_shape=jax.ShapeDtypeStruct((N_EXPERTS, SLOTS, D_MODEL), BF16),
        compiler_params=_cparams("arbitrary", "arbitrary"),
        name="expert_ffn",
    )(idx, h2, w1, w3, w2)


CMB_TD = 512
CMB_WIN = 256


def _combine_kernel(ws_ref, y_ref, pos_ref, aff_ref, x1_ref, g_ref, o_ref):
    tb = pl.program_id(1)
    pos = pos_ref[...]
    aff = aff_ref[...]
    lane = lax.broadcasted_iota(jnp.int32, (1, CMB_WIN), 1)
    acc = jnp.zeros((128, CMB_TD), F32)
    for e in range(N_EXPERTS):
        base = pl.multiple_of(jnp.minimum(ws_ref[tb, e] & -128, SLOTS - CMB_WIN), 128)
        onehot = jnp.where(pos[:, e:e + 1] == lane + base, 1.0, 0.0).astype(BF16)
        acc = acc + aff[:, e:e + 1] * _dot(onehot, y_ref[e, pl.ds(base, CMB_WIN), :])
    g = jnp.where(tb >= SEQ // 128, g_ref[1:2, :], g_ref[0:1, :])
    o_ref[...] = x1_ref[...] + g * acc


def combine(ws, y, pos, aff, x1, mod):
    nd = D_MODEL // CMB_TD
    return pl.pallas_call(
        _combine_kernel,
        grid_spec=pltpu.PrefetchScalarGridSpec(
            num_scalar_prefetch=1,
            grid=(nd, N_TB),
            in_specs=[pl.BlockSpec((N_EXPERTS, SLOTS, CMB_TD), lambda d, t, ws: (0, 0, d)),
                      pl.BlockSpec((128, 128), lambda d, t, ws: (t, 0)),
                      pl.BlockSpec((128, 128), lambda d, t, ws: (t, 0)),
                      pl.BlockSpec((128, CMB_TD), lambda d, t, ws: (t, d)),
                      pl.BlockSpec((8, CMB_TD), lambda d, t, ws: (0, 5 * nd + d))],
            out_specs=pl.BlockSpec((128, CMB_TD), lambda d, t, ws: (t, d))),
        out_shape=jax.ShapeDtypeStruct((T_ALL, D_MODEL), F32),
        compiler_params=_cparams("arbitrary", "arbitrary"),
        name="combine",
    )(ws, y, pos, aff, x1, mod)


HF_TM = 256
HY_COLS = HY_ORDER * HY_WIDTH


def _split2(a):
    hi = a.astype(BF16)
    return hi, (a - hi.astype(F32)).astype(BF16)


def _dot3(a, b):
    return _dot(a[0], b[0]) + (_dot(a[0], b[1]) + _dot(a[1], b[0]))


def _hy_filter_kernel(w1_ref, b1_ref, w2_ref, b2_ref, w3_ref, b3_ref, wo_ref, fr_ref, bands_ref, deltas_ref,
                      k_ref, sum_ref, *, seq):
    i = pl.program_id(0)
    n = i * HF_TM + lax.broadcasted_iota(jnp.int32, (HF_TM, 1), 0)
    lag = jnp.where(n < seq, n, jnp.where(n == seq, 0, 2 * seq - n)).astype(F32)
    t = lag / (seq - 1.0)
    ang = (2.0 * math.pi * lag) * bands_ref[...] / float(seq)
    lane = lax.broadcasted_iota(jnp.int32, (1, 128), 1)
    phase = jnp.where(lane > HY_BANDS, 0.5 * math.pi, 0.0)
    feats = jnp.where(lane == 0, t, jnp.where(lane <= 2 * HY_BANDS, jnp.cos(ang + phase), 0.0))
    fr = fr_ref[...]
    mm = lambda a, w_ref: _dot3(_split2(a), _split2(w_ref[...]))
    hdn = jnp.sin(fr * (mm(feats, w1_ref) + b1_ref[...]))
    hdn = jnp.sin(fr * (mm(hdn, w2_ref) + b2_ref[...]))
    hdn = jnp.sin(fr * (mm(hdn, w3_ref) + b3_ref[...]))
    h = mm(hdn, wo_ref) * jnp.exp(-t * deltas_ref[...])

    @pl.when(i == 0)
    def _():
        sum_ref[...] = jnp.zeros_like(sum_ref)

    sum_ref[0:1, :] += jnp.sum(jnp.abs(h), axis=0, keepdims=True)
    k_ref[...] = jnp.where(n == seq, 0.0, h)


def hy_filter(seq, w1, b1, w2, b2, w3, b3, wo, fr, bands, deltas):
    half = seq // HF_TM
    const = lambda i: (0, 0)
    sq = pl.BlockSpec((128, 128), const)
    row = pl.BlockSpec((1, 128), const)
    return pl.pallas_call(
        functools.partial(_hy_filter_kernel, seq=seq),
        grid=(2 * half,),
        in_specs=[sq, row, sq, row, sq, row,
                  pl.BlockSpec((128, HY_COLS), lambda i: (0, jnp.where(i >= half, 1, 0))),
                  row, row, pl.BlockSpec((1, HY_COLS), const)],
        out_specs=[pl.BlockSpec((HF_TM, HY_COLS), lambda i: (i, 0)),
                   pl.BlockSpec((8, HY_COLS), const)],
        out_shape=[jax.ShapeDtypeStruct((2 * seq, HY_COLS), F32),
                   jax.ShapeDtypeStruct((8, HY_COLS), F32)],
        compiler_params=_cparams("arbitrary"),
        name="hy_filter",
    )(w1, b1, w2, b2, w3, b3, wo, fr, bands, deltas)


FFT_R = 128
FFT_N = FFT_R * FFT_R
FFT_KH = 72
DFT_TN = 4096


def _dft1_kernel(x_ref, w_ref, re_ref, im_ref):
    res = _dot3(_split2(w_ref[...]), _split2(x_ref[...]))
    re_ref[...] = res[0:FFT_KH]
    im_ref[...] = res[FFT_KH:2 * FFT_KH]


def dft_rows(x2d, wmat):
    k, m = x2d.shape
    return pl.pallas_call(
        _dft1_kernel,
        grid=(m // DFT_TN,),
        in_specs=[pl.BlockSpec((k, DFT_TN), lambda j: (0, j)),
                  pl.BlockSpec((2 * FFT_KH, k), lambda j: (0, 0))],
        out_specs=[pl.BlockSpec((FFT_KH, DFT_TN), lambda j: (0, j))] * 2,
        out_shape=[jax.ShapeDtypeStruct((FFT_KH, m), F32)] * 2,
        compiler_params=_cparams("parallel"),
        name="hy_dft_rows",
    )(x2d, wmat)


def _twiddle(re, im, tc, ts, conj):
    if conj:
        return re * tc - im * ts, im * tc + re * ts
    return re * tc + im * ts, im * tc - re * ts


def _cdft(wp, re, im):
    res = _dot3(wp, _split2(jnp.concatenate([re, im], axis=0)))
    return res[0:FFT_R], res[FFT_R:2 * FFT_R]


FFT_G = 2


def _mid_filter_kernel(are_ref, aim_ref, twc_ref, tws_ref, wf_ref, wi_ref, kre_ref, kim_ref):
    wf = _split2(wf_ref[...])
    for g in range(FFT_G):
        re, im = _twiddle(are_ref[g], aim_ref[g], twc_ref[g], tws_ref[g], False)
        kre_ref[g], kim_ref[g] = _cdft(wf, re, im)


def _mid_data_kernel(are_ref, aim_ref, kre_ref, kim_ref, twc_ref, tws_ref, wf_ref, wi_ref, bre_ref, bim_ref):
    wf = _split2(wf_ref[...])
    wi = _split2(wi_ref[...])
    for g in range(FFT_G):
        tc = twc_ref[g]
        ts = tws_ref[g]
        re, im = _twiddle(are_ref[g], aim_ref[g], tc, ts, False)
        xr, xi = _cdft(wf, re, im)
        kr = kre_ref[g]
        ki = kim_ref[g]
        br, bi = _cdft(wi, xr * kr - xi * ki, xr * ki + xi * kr)
        bre_ref[g], bim_ref[g] = _twiddle(br, bi, tc, ts, True)


def fft_mid(are, aim, twc, tws, cmat, smat, kf=None, order=0):
    width = are.shape[-1]
    slab = pl.BlockSpec((FFT_G, FFT_R, width), lambda k1: (k1, 0, 0))
    tw = pl.BlockSpec((FFT_G, FFT_R, 1), lambda k1: (k1, 0, 0))
    sq = pl.BlockSpec((2 * FFT_R, 2 * FFT_R), lambda k1: (0, 0))
    if kf is None:
        body, ins, specs = _mid_filter_kernel, (are, aim, twc, tws, cmat, smat), [slab, slab, tw, tw, sq, sq]
    else:
        kslab = pl.BlockSpec((FFT_G, FFT_R, width), lambda k1: (k1, 0, order))
        body, ins = _mid_data_kernel, (are, aim, kf[0], kf[1], twc, tws, cmat, smat)
        specs = [slab, slab, kslab, kslab, tw, tw, sq, sq]
    return pl.pallas_call(
        body,
        grid=(FFT_KH // FFT_G,),
        in_specs=specs,
        out_specs=[slab, slab],
        out_shape=[jax.ShapeDtypeStruct((FFT_KH, FFT_R, width), F32)] * 2,
        compiler_params=_cparams("parallel"),
        name="hy_fft_mid_filter" if kf is None else "hy_fft_mid",
    )(*ins)


def _idft1_kernel(bre_ref, bim_ref, w_ref, gate_ref, zin_ref, bias_ref, sum_ref, o_ref):
    pad = jnp.zeros((2 * FFT_R - 2 * FFT_KH, DFT_TN), F32)
    b = jnp.concatenate([bre_ref[...], bim_ref[...], pad], axis=0)
    conv = _dot3(_split2(w_ref[...]), _split2(b))
    o_ref[...] = gate_ref[...] * (conv / (FFT_N * sum_ref[...]) + bias_ref[...] * zin_ref[...])


def idft_rows_gated(bre, bim, wmat, gate, zin, bias_t, sum_t):
    rows, m = gate.shape
    const = lambda j: (0, 0)
    tile = lambda r: pl.BlockSpec((r, DFT_TN), lambda j: (0, j))
    return pl.pallas_call(
        _idft1_kernel,
        grid=(m // DFT_TN,),
        in_specs=[tile(FFT_KH), tile(FFT_KH), pl.BlockSpec((rows, 2 * FFT_R), const),
                  tile(rows), tile(rows), pl.BlockSpec((1, DFT_TN), const), pl.BlockSpec((1, DFT_TN), const)],
        out_specs=tile(rows),
        out_shape=jax.ShapeDtypeStruct((rows, m), F32),
        compiler_params=_cparams("parallel"),
        name="hy_idft_rows",
    )(bre, bim, wmat, gate, zin, bias_t, sum_t)


CTX_N = 2 * CTX_LEN


def _hy_ctx_kernel(u_ref, k_ref, c_ref, s_ref, gate_ref, bias_ref, sum_ref, o_ref):
    cp = _split2(c_ref[...])
    sp = _split2(s_ref[...])
    head = lambda pair: (pair[0][:, 0:CTX_LEN], pair[1][:, 0:CTX_LEN])
    top = lambda pair: (pair[0][0:CTX_LEN, :], pair[1][0:CTX_LEN, :])
    u = u_ref[...]
    up = _split2(u)
    ur = _dot3(head(cp), up)
    ui = -_dot3(head(sp), up)
    kp = _split2(k_ref[...])
    kr = _dot3(cp, kp)
    ki = -_dot3(sp, kp)
    conv = (_dot3(top(cp), _split2(ur * kr - ui * ki)) - _dot3(top(sp), _split2(ur * ki + ui * kr)))
    o_ref[...] = gate_ref[...] * (conv / (CTX_N * sum_ref[...]) + bias_ref[...] * u)


def hy_ctx_conv(u, kfilt, order, cmat, smat, gate, bias, sums):
    const = lambda i: (0, 0)
    chan = pl.BlockSpec((CTX_LEN, HY_WIDTH), const)
    vec = pl.BlockSpec((1, HY_WIDTH), const)
    return pl.pallas_call(
        _hy_ctx_kernel,
        grid=(1,),
        in_specs=[chan, pl.BlockSpec((CTX_N, HY_WIDTH), lambda i: (0, order)),
                  pl.BlockSpec((CTX_N, CTX_N), const), pl.BlockSpec((CTX_N, CTX_N), const), chan, vec, vec],
        out_specs=chan,
        out_shape=jax.ShapeDtypeStruct((CTX_LEN, HY_WIDTH), F32),
        compiler_params=_cparams("arbitrary"),
        name="hy_ctx_conv",
    )(u, kfilt, cmat, smat, gate, bias, sums)


def _rope_tables():
    rows = SEQ // GRID_W
    r, col = jnp.meshgrid(jnp.arange(rows), jnp.arange(GRID_W), indexing='ij')
    per_axis = MLA_ROPE // 2
    inv = ROPE_THETA ** (-jnp.arange(0, per_axis, 2, dtype=F32) / per_axis)
    ang = jnp.concatenate([r.reshape(-1, 1) * inv, col.reshape(-1, 1) * inv], axis=-1)
    cos = jnp.concatenate([jnp.cos(ang), jnp.ones((CTX_LEN, per_axis), F32)], axis=0)
    sin = jnp.concatenate([jnp.sin(ang), jnp.zeros((CTX_LEN, per_axis), F32)], axis=0)
    z = jnp.zeros_like(cos)
    return jnp.concatenate([cos, z, cos, z], axis=-1), jnp.concatenate([-sin, z, sin, z], axis=-1)


def _permute_w_in(w_in):
    z, xbc, dt, cq, ckv, kr, hy = jnp.split(w_in, np.cumsum(
        (SSD_WIDTH, SSD_XBC, 2 * SSD_HEADS, MLA_Q_RANK, MLA_KV_RANK, MLA_ROPE)).tolist(), axis=-1)
    zeros = lambda n: jnp.zeros(w_in.shape[:-1] + (n,), w_in.dtype)
    return jnp.concatenate([hy, z, ckv, xbc, kr[..., :32], zeros(32), kr[..., 32:], zeros(32),
                            dt, zeros(128 - 2 * SSD_HEADS), cq], axis=-1).astype(BF16)


def _pad_w_uq(w_uq):
    w = w_uq.reshape(DEPTH, MLA_Q_RANK, MLA_HEADS, MLA_QK)
    zeros = jnp.zeros(w.shape[:-1] + (32,), w.dtype)
    w = jnp.concatenate([w[..., :MLA_NOPE], w[..., MLA_NOPE:MLA_NOPE + 32], zeros,
                         w[..., MLA_NOPE + 32:], zeros], axis=-1)
    return w.reshape(DEPTH, MLA_Q_RANK, MLA_HEADS * QK_PAD).astype(BF16)


def kernel(x, c, ctx, c_ctx, w_mod, b_mod, norm1_w, norm2_w, w_in, w_out, ssd_conv_w, ssd_conv_b,
           ssd_dt_bias, ssd_a_log, ssd_d, ssd_norm_w, mla_q_norm_w, mla_w_uq, mla_kv_norm_w, mla_w_ukv,
           hy_short_w, hy_short_b, hy_w1, hy_b1, hy_w2, hy_b2, hy_w3, hy_b3, hy_w_out, hy_freq, hy_bias,
           moe_router, moe_w1, moe_w3, moe_w2, final_norm_w):
    xj = jnp.concatenate([x[0], ctx[0]], axis=0)
    cvec = jnp.concatenate([c, c_ctx[None, :], jnp.zeros((6, D_MODEL), F32)], axis=0)
    mods = modulation(cvec, w_mod, b_mod)
    c2, s2 = _rope_tables()
    w_in_p = _permute_w_in(w_in)
    w_uq_p = _pad_w_uq(mla_w_uq)
    w_ukv_b = mla_w_ukv.astype(BF16)
    w_out_b = w_out.astype(BF16)
    router_p = jnp.pad(moe_router, ((0, 0), (0, 0), (0, 128 - N_EXPERTS)))
    lane_pad = lambda a: jnp.pad(a, ((0, 0), (0, 0), (0, 128 - SSD_HEADS)))[:, :, None, :]
    dt_bias = lane_pad(ssd_dt_bias)
    a_log = lane_pad(ssd_a_log)
    d_skip = jnp.repeat(ssd_d, SSD_HEAD_DIM, axis=-1)[:, None, :]
    tri_lower = jnp.asarray(np.tril(np.ones((SSD_CHUNK, SSD_CHUNK), np.float32)), BF16)
    tri_upper = jnp.asarray(np.triu(np.ones((SSD_CHUNK, SSD_CHUNK), np.float32)), BF16)
    lstrict = jnp.asarray(np.tril(np.ones((ROUTE_TB, ROUTE_TB), np.float32), -1), BF16)
    tv = np.zeros((128, 128), np.float32)
    tv[:, 0] = np.arange(128)
    tv[:, 1] = 1.0
    tvals = jnp.asarray(tv, BF16)

    pad_to = lambda a, r, c_: jnp.pad(a, ((0, 0), (0, r - a.shape[1]), (0, c_ - a.shape[2])))
    hf_w1, hf_w2, hf_w3 = pad_to(hy_w1, 128, 128), pad_to(hy_w2, 128, 128), pad_to(hy_w3, 128, 128)
    hf_b1, hf_b2, hf_b3, hf_fr = (pad_to(a[:, None, :], 1, 128) for a in (hy_b1, hy_b2, hy_b3, hy_freq))
    hf_wo = hy_w_out.reshape(DEPTH, HY_FFN, HY_ORDER, 2, HY_WIDTH).transpose(0, 1, 3, 2, 4)
    hf_wo = pad_to(hf_wo.reshape(DEPTH, HY_FFN, 2 * HY_COLS), 128, 2 * HY_COLS)
    band_vals = jnp.linspace(1e-4, HY_BANDS - 1, HY_BANDS, dtype=F32)
    bands = jnp.concatenate([jnp.zeros((1,), F32), band_vals, band_vals,
                             jnp.zeros((128 - HY_EMB,), F32)])[None, :]
    deltas = jnp.abs(jnp.linspace(math.log(HY_TARGET) / HY_SLOW_DECAY_PCT,
                                  math.log(HY_TARGET) / HY_FAST_DECAY_PCT, HY_WIDTH, dtype=F32))
    deltas = jnp.tile(deltas, HY_ORDER)[None, :]
    jk = np.outer(np.arange(FFT_R), np.arange(FFT_R))
    cos_r = np.cos(2 * np.pi * (jk % FFT_R) / FFT_R)
    sin_r = np.sin(2 * np.pi * (jk % FFT_R) / FFT_R)
    half = FFT_R // 2
    kh = FFT_KH
    dft_in_full = jnp.asarray(np.concatenate([cos_r[:kh], -sin_r[:kh]], axis=0), F32)
    dft_in_half = jnp.asarray(np.concatenate([cos_r[:kh, :half], -sin_r[:kh, :half]], axis=0), F32)
    dft_fwd = jnp.asarray(np.block([[cos_r, sin_r], [-sin_r, cos_r]]), F32)
    dft_inv = jnp.asarray(np.block([[cos_r, -sin_r], [sin_r, cos_r]]), F32)
    pair = np.zeros((kh,))
    pair[0] = pair[half] = 1.0
    pair[1:half] = 2.0
    dft_out_half = jnp.asarray(np.concatenate([cos_r[:half, :kh] * pair, -sin_r[:half, :kh] * pair,
                                               np.zeros((half, 2 * FFT_R - 2 * kh))], axis=1), F32)
    tw_cos = jnp.asarray(np.cos(2 * np.pi * jk[:kh] / FFT_N), F32)[:, :, None]
    tw_sin = jnp.asarray(np.sin(2 * np.pi * jk[:kh] / FFT_N), F32)[:, :, None]
    jk_c = np.outer(np.arange(CTX_N), np.arange(CTX_N)) % CTX_N
    cos_c = jnp.asarray(np.cos(2 * np.pi * jk_c / CTX_N), F32)
    sin_c = jnp.asarray(np.sin(2 * np.pi * jk_c / CTX_N), F32)

    for i in range(DEPTH):
        mod = mods[i]
        p = in_projection(xj, mod, norm1_w[i][None, :], w_in_p[i])
        q, k, v = mla_projection(p, c2, s2, mla_q_norm_w[i][None, :], mla_kv_norm_w[i][None, :],
                                 w_uq_p[i], w_ukv_b[i])
        y_mla = attention(q, k, v)

        xbc = dwconv(p, P_XBC // SSD_XBC, SSD_XBC, ssd_conv_w[i], ssd_conv_b[i][None, :], act=True)
        y_f = ssd_scan(xbc, p, dt_bias[i, 0], a_log[i, 0], tri_lower, reverse=False)
        y_ssd = ssd_scan(xbc, p, dt_bias[i, 1], a_log[i, 1], tri_upper, reverse=True, yf=y_f,
                         dskip=d_skip[i], nw=ssd_norm_w[i][None, :])

        vxx = [dwconv(p, j, HY_WIDTH, hy_short_w[i][:, j * HY_WIDTH:(j + 1) * HY_WIDTH],
                      hy_short_b[i][None, j * HY_WIDTH:(j + 1) * HY_WIDTH], act=False) for j in range(3)]
        filt = (hf_w1[i], hf_b1[i], hf_w2[i], hf_b2[i], hf_w3[i], hf_b3[i], hf_wo[i], hf_fr[i], bands, deltas)
        k_l, sum_l = hy_filter(SEQ, *filt)
        k_c, sum_c = hy_filter(CTX_LEN, *filt)
        are, aim = dft_rows(k_l.reshape(FFT_R, FFT_R * HY_COLS), dft_in_full)
        kf = fft_mid(are.reshape(kh, FFT_R, HY_COLS), aim.reshape(kh, FFT_R, HY_COLS),
                     tw_cos, tw_sin, dft_fwd, dft_inv)
        z_l = vxx[0][:SEQ].reshape(half, FFT_R * HY_WIDTH)
        z_c = vxx[0][SEQ:]
        for o in range(HY_ORDER):
            ch = slice(o * HY_WIDTH, (o + 1) * HY_WIDTH)
            are, aim = dft_rows(z_l, dft_in_half)
            bre, bim = fft_mid(are.reshape(kh, FFT_R, HY_WIDTH), aim.reshape(kh, FFT_R, HY_WIDTH),
                               tw_cos, tw_sin, dft_fwd, dft_inv, kf=kf, order=o)
            z_l = idft_rows_gated(bre.reshape(kh, FFT_R * HY_WIDTH), bim.reshape(kh, FFT_R * HY_WIDTH),
                                  dft_out_half,
                                  vxx[1 + o][:SEQ].reshape(half, FFT_R * HY_WIDTH), z_l,
                                  jnp.tile(hy_bias[i, o][None, :], (1, DFT_TN // HY_WIDTH)),
                                  jnp.tile(sum_l[0:1, ch], (1, DFT_TN // HY_WIDTH)))
            z_c = hy_ctx_conv(z_c, k_c, o, cos_c, sin_c, vxx[1 + o][SEQ:], hy_bias[i, o][None, :], sum_c[0:1, ch])
        y_hy = jnp.concatenate([z_l.reshape(SEQ, HY_WIDTH), z_c], axis=0)

        x1, h2, aff = out_projection(y_ssd, y_mla, y_hy, xj, mod, norm2_w[i][None, :], w_out_b[i], router_p[i])
        pos, ws = route(aff, lstrict)
        found = compact(ws, pos, tvals)
        idx = (found[:, :SLOTS, 0] + found[:, :SLOTS, 1]).astype(jnp.int32)
        y_moe = expert_ffn(idx, h2, moe_w1, moe_w3, moe_w2, i)
        xj = combine(ws, y_moe, pos, aff, x1, mod)
    return final_norm(xj, final_norm_w[None, :])[None]
```
